```python
import jax, jax.numpy as jnp
from jax import lax
import numpy as np

D_MODEL = 1024
BATCH = 4
SEQ = 8192
DEPTH = 2

GRID_W = 64
CTX_LEN = 256
Q_BLOCK = 128
ROPE_THETA = 10000.0
EPS = 1e-6
N_MOD = 6
MLA_HEADS = 8
MLA_NOPE = 64
MLA_ROPE = 32
MLA_V = 64
MLA_Q_RANK = 384
MLA_KV_RANK = 256
POOL_WINDOWS = (2, 4, 8, 16)
POOL_WIDTH = 512
POOL_GROUP = POOL_WIDTH // len(POOL_WINDOWS)
MIX0_IN = MLA_Q_RANK + MLA_KV_RANK + MLA_ROPE + POOL_WIDTH
MIX0_OUT = MLA_HEADS * MLA_V + POOL_WIDTH
GQA_HEADS = 8
GQA_KV_HEADS = 2
GQA_HEAD_DIM = 128
GQA_GROUP = GQA_HEADS // GQA_KV_HEADS
GQA_Q_W = GQA_HEADS * GQA_HEAD_DIM
GQA_KV_W = GQA_KV_HEADS * GQA_HEAD_DIM
D_FF = 2816
CONV_W = 3
N_EVEN = (DEPTH + 1) // 2
N_ODD = DEPTH // 2

kernel_name = 'hybrid_mla_pool_gqa_convffn_dit'


def rms_norm(x, gain=None):
    xf = x.astype(jnp.float32)
    y = (xf * lax.rsqrt(jnp.mean(xf * xf, axis=-1, keepdims=True) + EPS)).astype(x.dtype)
    return y if gain is None else y * gain


def modulate(x, shift, scale):
    return rms_norm(x) * (1.0 + scale) + shift


def axial_rope_tables(n_tokens, rope_dim):
    rows = n_tokens // GRID_W
    row = jnp.repeat(jnp.arange(rows, dtype=jnp.float32), GRID_W)
    col = jnp.tile(jnp.arange(GRID_W, dtype=jnp.float32), rows)
    n_freq = rope_dim // 4
    freq = ROPE_THETA ** (-jnp.arange(n_freq, dtype=jnp.float32) / n_freq)
    ang = jnp.concatenate([row[:, None] * freq, col[:, None] * freq], axis=-1)
    return jnp.cos(ang), jnp.sin(ang)


def apply_rope(x, cos, sin):
    xr = x.reshape(x.shape[:-1] + (x.shape[-1] // 2, 2))
    x0, x1 = xr[..., 0], xr[..., 1]
    c = cos[None, :, None, :].astype(x.dtype)
    s = sin[None, :, None, :].astype(x.dtype)
    return jnp.stack([x0 * c - x1 * s, x0 * s + x1 * c], axis=-1).reshape(x.shape)


def attention(q, k, v):
    B, Nq, KV, G, Dh = q.shape
    blk = min(Q_BLOCK, Nq)
    qb = jnp.moveaxis(q.reshape(B, Nq // blk, blk, KV, G, Dh), 1, 0)
    scale = Dh ** -0.5

    def one_block(qi):
        s = jnp.einsum('bqhgd,bkhd->bhgqk', qi, k).astype(jnp.float32) * scale
        p = jax.nn.softmax(s, axis=-1).astype(v.dtype)
        return jnp.einsum('bhgqk,bkhd->bqhgd', p, v)

    o = lax.map(one_block, qb)
    return jnp.moveaxis(o, 0, 1).reshape(B, Nq, KV * G * v.shape[-1])


def multiscale_pool(p, w_pool, s_pool):
    B, T, C = p.shape
    cs = jnp.cumsum(p.astype(jnp.float32), axis=1)
    cs = jnp.concatenate([jnp.zeros((B, 1, C), jnp.float32), cs], axis=1)
    t = jnp.arange(T)
    outs = []
    for g, w in enumerate(POOL_WINDOWS):
        lo = jnp.clip(t - w // 2, 0, T)
        hi = jnp.clip(t - w // 2 + w, 0, T)
        csg = cs[:, :, g * POOL_GROUP:(g + 1) * POOL_GROUP]
        mean = (csg[:, hi] - csg[:, lo]) / (hi - lo).astype(jnp.float32)[None, :, None]
        d = mean.astype(p.dtype) - p[:, :, g * POOL_GROUP:(g + 1) * POOL_GROUP]
        outs.append(d @ w_pool[g])
    return jnp.concatenate(outs, axis=-1) * s_pool


def mla_q(cq, g_q, w_uq, cos, sin, rope):
    B, T, _ = cq.shape
    q = (rms_norm(cq, g_q) @ w_uq).reshape(B, T, MLA_HEADS, MLA_NOPE + MLA_ROPE)
    if rope:
        q = jnp.concatenate([q[..., :MLA_NOPE], apply_rope(q[..., MLA_NOPE:], cos, sin)], axis=-1)
    return q[:, :, :, None, :]


def mla_kv(ckv, kr, g_kv, w_uk, w_uv, cos, sin, rope):
    B, T, _ = ckv.shape
    ckv = rms_norm(ckv, g_kv)
    k_nope = (ckv @ w_uk).reshape(B, T, MLA_HEADS, MLA_NOPE)
    v = (ckv @ w_uv).reshape(B, T, MLA_HEADS, MLA_V)
    kr = kr[:, :, None, :]
    if rope:
        kr = apply_rope(kr, cos, sin)
    k = jnp.concatenate([k_nope, jnp.broadcast_to(kr, (B, T, MLA_HEADS, MLA_ROPE))], axis=-1)
    return k, v


def mla_pool_mixer(hc, hl, w_in, g_q, w_uq, g_kv, w_uk, w_uv, w_pool, s_pool, w_out, cos, sin, need_ctx):
    i_kv = MLA_Q_RANK
    i_kr = i_kv + MLA_KV_RANK
    i_p = i_kr + MLA_ROPE
    pl = hl @ w_in
    ql = mla_q(pl[..., :i_kv], g_q, w_uq, cos, sin, True)
    kl, vl = mla_kv(pl[..., i_kv:i_kr], pl[..., i_kr:i_p], g_kv, w_uk, w_uv, cos, sin, True)
    if need_ctx:
        pc = hc @ w_in
        qc = mla_q(pc[..., :i_kv], g_q, w_uq, cos, sin, False)
        kc, vc = mla_kv(pc[..., i_kv:i_kr], pc[..., i_kr:i_p], g_kv, w_uk, w_uv, cos, sin, False)
    else:
        pkv = hc @ w_in[:, i_kv:i_p]
        kc, vc = mla_kv(pkv[..., :MLA_KV_RANK], pkv[..., MLA_KV_RANK:], g_kv, w_uk, w_uv, cos, sin, False)
    al = attention(ql, jnp.concatenate([kc, kl], axis=1), jnp.concatenate([vc, vl], axis=1))
    ol = jnp.concatenate([al, multiscale_pool(pl[..., i_p:], w_pool, s_pool)], axis=-1) @ w_out
    oc = None
    if need_ctx:
        ac = attention(qc, kc, vc)
        oc = jnp.concatenate([ac, multiscale_pool(pc[..., i_p:], w_pool, s_pool)], axis=-1) @ w_out
    return oc, ol


def gqa_q(a, g_q, cos, sin, rope):
    B, T, _ = a.shape
    q = rms_norm(a.reshape(B, T, GQA_HEADS, GQA_HEAD_DIM), g_q)
    if rope:
        q = apply_rope(q, cos, sin)
    return q.reshape(B, T, GQA_KV_HEADS, GQA_GROUP, GQA_HEAD_DIM)


def gqa_kv(a, g_k, cos, sin, rope):
    B, T, _ = a.shape
    k = rms_norm(a[..., :GQA_KV_W].reshape(B, T, GQA_KV_HEADS, GQA_HEAD_DIM), g_k)
    v = a[..., GQA_KV_W:].reshape(B, T, GQA_KV_HEADS, GQA_HEAD_DIM)
    if rope:
        k = apply_rope(k, cos, sin)
    return k, v


def gqa_mixer(hc, hl, w_in, g_q, g_k, w_out, cos, sin, need_ctx):
    pl = hl @ w_in
    ql = gqa_q(pl[..., :GQA_Q_W], g_q, cos, sin, True)
    kl, vl = gqa_kv(pl[..., GQA_Q_W:], g_k, cos, sin, True)
    if need_ctx:
        pc = hc @ w_in
        qc = gqa_q(pc[..., :GQA_Q_W], g_q, cos, sin, False)
        kc, vc = gqa_kv(pc[..., GQA_Q_W:], g_k, cos, sin, False)
    else:
        kc, vc = gqa_kv(hc @ w_in[:, GQA_Q_W:], g_k, cos, sin, False)
    ol = attention(ql, jnp.concatenate([kc, kl], axis=1), jnp.concatenate([vc, vl], axis=1)) @ w_out
    oc = attention(qc, kc, vc) @ w_out if need_ctx else None
    return oc, ol


def conv_ffn(h, w_up, conv_w, conv_b, w_down):
    a = h @ w_up
    g, u = a[..., :D_FF], a[..., D_FF:]
    T = g.shape[1]
    half = CONV_W // 2
    gp = jnp.pad(g, ((0, 0), (half, half), (0, 0)))
    acc = conv_b + gp[:, 0:T] * conv_w[0]
    for j in range(1, CONV_W):
        acc = acc + gp[:, j:j + T] * conv_w[j]
    return (jax.nn.silu(acc) * u) @ w_down


def setup_inputs(seed: int = 0) -> dict:
    key = jax.random.key(seed)
    ks = iter(jax.random.split(key, 32))
    D = D_MODEL

    def normal(shape, scale=1.0):
        return jax.random.normal(next(ks), shape, jnp.float32) * scale

    def gain(shape):
        return 1.0 + 0.05 * normal(shape)

    return {
        'x': normal((BATCH, SEQ, D)),
        'c': normal((BATCH, D)),
        'ctx': normal((BATCH, CTX_LEN, D)),
        'c_ctx': normal((D,)),
        'w_mod': normal((DEPTH, D, N_MOD * D), D ** -0.5),
        'b_mod': normal((DEPTH, N_MOD * D), 0.02),
        'mix0_w_in': normal((N_EVEN, D, MIX0_IN), D ** -0.5),
        'mla_g_q': gain((N_EVEN, MLA_Q_RANK)),
        'mla_w_uq': normal((N_EVEN, MLA_Q_RANK, MLA_HEADS * (MLA_NOPE + MLA_ROPE)), MLA_Q_RANK ** -0.5),
        'mla_g_kv': gain((N_EVEN, MLA_KV_RANK)),
        'mla_w_uk': normal((N_EVEN, MLA_KV_RANK, MLA_HEADS * MLA_NOPE), MLA_KV_RANK ** -0.5),
        'mla_w_uv': normal((N_EVEN, MLA_KV_RANK, MLA_HEADS * MLA_V), MLA_KV_RANK ** -0.5),
        'pool_w': normal((N_EVEN, len(POOL_WINDOWS), POOL_GROUP, POOL_GROUP), POOL_GROUP ** -0.5),
        'pool_scale': gain((N_EVEN, POOL_WIDTH)),
        'mix0_w_out': normal((N_EVEN, MIX0_OUT, D), MIX0_OUT ** -0.5),
        'gqa_w_in': normal((N_ODD, D, GQA_Q_W + 2 * GQA_KV_W), D ** -0.5),
        'gqa_g_q': gain((N_ODD, GQA_HEAD_DIM)),
        'gqa_g_k': gain((N_ODD, GQA_HEAD_DIM)),
        'gqa_w_out': normal((N_ODD, GQA_Q_W, D), GQA_Q_W ** -0.5),
        'ffn_w_up': normal((DEPTH, D, 2 * D_FF), D ** -0.5),
        'ffn_conv_w': normal((DEPTH, CONV_W, D_FF), CONV_W ** -0.5),
        'ffn_conv_b': normal((DEPTH, D_FF), 0.02),
        'ffn_w_down': normal((DEPTH, D_FF, D), D_FF ** -0.5),
        'g_final': gain((D,)),
    }


def reference(x, c, ctx, c_ctx, w_mod, b_mod, mix0_w_in, mla_g_q, mla_w_uq, mla_g_kv, mla_w_uk, mla_w_uv,
              pool_w, pool_scale, mix0_w_out, gqa_w_in, gqa_g_q, gqa_g_k, gqa_w_out,
              ffn_w_up, ffn_conv_w, ffn_conv_b, ffn_w_down, g_final):
    B, N, D = x.shape
    cos_a, sin_a = axial_rope_tables(N, MLA_ROPE)
    cos_c, sin_c = axial_rope_tables(N, GQA_HEAD_DIM)
    xl, xc = x, ctx
    for i in range(DEPTH):
        last = i == DEPTH - 1
        j = i // 2
        ml = (jax.nn.silu(c) @ w_mod[i] + b_mod[i]).reshape(B, N_MOD, 1, D)
        mc = (jax.nn.silu(c_ctx) @ w_mod[i] + b_mod[i]).reshape(N_MOD, D)
        hl = modulate(xl, ml[:, 0], ml[:, 1])
        hc = modulate(xc, mc[0], mc[1])
        if i % 2 == 0:
            oc, ol = mla_pool_mixer(hc, hl, mix0_w_in[j], mla_g_q[j], mla_w_uq[j], mla_g_kv[j], mla_w_uk[j],
                                    mla_w_uv[j], pool_w[j], pool_scale[j], mix0_w_out[j], cos_a, sin_a, not last)
        else:
            oc, ol = gqa_mixer(hc, hl, gqa_w_in[j], gqa_g_q[j], gqa_g_k[j], gqa_w_out[j], cos_c, sin_c, not last)
        xl = xl + ml[:, 2] * ol
        xl = xl + ml[:, 5] * conv_ffn(modulate(xl, ml[:, 3], ml[:, 4]),
                                      ffn_w_up[i], ffn_conv_w[i], ffn_conv_b[i], ffn_w_down[i])
        if not last:
            xc = xc + mc[2] * oc
            xc = xc + mc[5] * conv_ffn(modulate(xc, mc[3], mc[4]),
                                       ffn_w_up[i], ffn_conv_w[i], ffn_conv_b[i], ffn_w_down[i])
    return rms_norm(xl, g_final)
```

```python
import functools

import jax
import jax.numpy as jnp
import numpy as np
from jax import lax
from jax.experimental import pallas as pl
from jax.experimental.pallas import tpu as pltpu

D_MODEL = 1024
SEQ = 8192
GRID_W = 64
CTX_LEN = 256
T_ALL = SEQ + CTX_LEN
ROPE_THETA = 10000.0
EPS = 1e-6
N_MOD = 6
MLA_HEADS = 8
MLA_NOPE = 64
MLA_ROPE = 32
MLA_V = 64
MLA_Q_RANK = 384
MLA_KV_RANK = 256
POOL_WINDOWS = (2, 4, 8, 16)
POOL_WIDTH = 512
GQA_HEADS = 8
GQA_KV_HEADS = 2
GQA_HEAD_DIM = 128
GQA_GROUP = GQA_HEADS // GQA_KV_HEADS
D_FF = 2816
CONV_W = 3

LANES = 128
SUBLANES = 8
ROW_TILE = 256
N_LAT_TILES = SEQ // ROW_TILE
N_TILES = T_ALL // ROW_TILE
HALO_BLOCKS_PER_TILE = ROW_TILE // SUBLANES
KV_CHUNK = 768
FF_CHUNK = 256
VMEM_LIMIT = 56 * 1024 * 1024

MIX0_IN_PAD = MLA_Q_RANK + MLA_KV_RANK + LANES + POOL_WIDTH
HEADS_W = MLA_HEADS * LANES


def _params(n_axes):
    return pltpu.CompilerParams(dimension_semantics=("arbitrary",) * n_axes, vmem_limit_bytes=VMEM_LIMIT)


def _rms(x):
    return x * lax.rsqrt(jnp.mean(x * x, axis=-1, keepdims=True) + EPS)


def _dot(a, b):
    return jnp.dot(a, b, preferred_element_type=jnp.float32)


def _bf(x):
    return x.astype(jnp.bfloat16)


def _mod_kernel(c_ref, w_ref, b_ref, o_ref):
    s = jax.nn.silu(c_ref[...])
    o_ref[0] = jnp.dot(s, w_ref[0], preferred_element_type=jnp.float32,
                       precision=lax.Precision.HIGHEST) + b_ref[0]


def _modulation(cc, w_mod, b_mod):
    depth = w_mod.shape[0]
    return pl.pallas_call(
        _mod_kernel,
        out_shape=jax.ShapeDtypeStruct((depth, SUBLANES, N_MOD * D_MODEL), jnp.float32),
        grid=(depth, N_MOD),
        in_specs=[
            pl.BlockSpec((SUBLANES, D_MODEL), lambda i, j: (0, 0)),
            pl.BlockSpec((1, D_MODEL, D_MODEL), lambda i, j: (i, 0, j)),
            pl.BlockSpec((1, 1, D_MODEL), lambda i, j: (i, 0, j)),
        ],
        out_specs=pl.BlockSpec((1, SUBLANES, D_MODEL), lambda i, j: (i, 0, j)),
        compiler_params=_params(2),
        name="modulation",
    )(cc, w_mod, b_mod.reshape(depth, 1, N_MOD * D_MODEL))


def _row_spec(width, rows=ROW_TILE):
    return pl.BlockSpec((1, rows, width), lambda b, t: (b, t, 0))


def _mod_spec():
    return pl.BlockSpec((1, 1, N_MOD, D_MODEL), lambda b, t: (b, t // N_LAT_TILES, 0, 0))


def _const_spec(shape):
    nd = len(shape)
    return pl.BlockSpec(shape, lambda b, t: (0,) * nd)


def _table_spec():
    return pl.BlockSpec((ROW_TILE, LANES), lambda b, t: (t, 0))


def _prev_halo_spec(width):
    return pl.BlockSpec((1, SUBLANES, width),
                        lambda b, t: (b, jnp.maximum(t * HALO_BLOCKS_PER_TILE - 1, 0), 0))


def _next_halo_spec(width, total_rows=T_ALL):
    last = total_rows // SUBLANES - 1
    return pl.BlockSpec((1, SUBLANES, width),
                        lambda b, t: (b, jnp.minimum((t + 1) * HALO_BLOCKS_PER_TILE, last), 0))


def _halo_valid(t):
    prev_ok = jnp.logical_and(t != 0, t != N_LAT_TILES)
    next_ok = jnp.logical_and(t != N_LAT_TILES - 1, t != N_LAT_TILES)
    return prev_ok, next_ok


def _with_halo(prev_ref, main_ref, next_ref, t):
    prev_ok, next_ok = _halo_valid(t)
    return jnp.concatenate([prev_ref[0], main_ref[0], next_ref[0]], axis=0), prev_ok, next_ok


def _halo_row_mask(prev_ok, next_ok):
    r = lax.broadcasted_iota(jnp.int32, (ROW_TILE + 2 * SUBLANES, 1), 0)
    return jnp.logical_and(jnp.logical_or(r >= SUBLANES, prev_ok),
                           jnp.logical_or(r < ROW_TILE + SUBLANES, next_ok))


def _shift_rows(x, k):
    n = x.shape[0]
    return pltpu.roll(x, k % n, axis=0)[SUBLANES:SUBLANES + ROW_TILE]


def _rope_mla(x, c, s_up, s_dn):
    return x * c + pltpu.roll(x, MLA_ROPE // 2, axis=1) * s_up + pltpu.roll(x, LANES - MLA_ROPE // 2, axis=1) * s_dn


def _in0_kernel(x_ref, mod_ref, w_in_ref, gq_ref, w_uq_ref, gkv_ref, w_uk_ref, w_uv_ref,
                c_ref, su_ref, sd_ref, q_ref, k_ref, v_ref, p_ref):
    m = mod_ref[0, 0]
    h = _bf(_rms(x_ref[0]) * (1.0 + m[1:2]) + m[0:1])
    a = _dot(h, w_in_ref[...])
    i_kv = MLA_Q_RANK
    i_kr = i_kv + MLA_KV_RANK
    i_p = i_kr + LANES
    c, su, sd = c_ref[...], su_ref[...], sd_ref[...]
    scale = (MLA_NOPE + MLA_ROPE) ** -0.5

    q = _dot(_bf(_rms(a[:, :i_kv]) * gq_ref[...]), w_uq_ref[...])
    ckv = _bf(_rms(a[:, i_kv:i_kr]) * gkv_ref[...])
    kn = _dot(ckv, w_uk_ref[...])
    kr = _rope_mla(a[:, i_kr:i_p], c, su, sd)
    for hd in range(MLA_HEADS):
        sl = slice(hd * LANES, (hd + 1) * LANES)
        q_ref[0, :, sl] = _bf(_rope_mla(q[:, sl], c, su, sd) * scale)
        k_ref[0, :, sl] = _bf(kn[:, sl] + kr)
    v_ref[0] = _bf(_dot(ckv, w_uv_ref[...]))
    p_ref[0] = a[:, i_p:]


def _in0(xa, mods, w_in, g_q, w_uq, g_kv, w_uk, w_uv, tabs):
    B = xa.shape[0]
    outs = (
        jax.ShapeDtypeStruct((B, T_ALL, HEADS_W), jnp.bfloat16),
        jax.ShapeDtypeStruct((B, T_ALL, HEADS_W), jnp.bfloat16),
        jax.ShapeDtypeStruct((B, T_ALL, HEADS_W), jnp.bfloat16),
        jax.ShapeDtypeStruct((B, T_ALL, POOL_WIDTH), jnp.float32),
    )
    return pl.pallas_call(
        _in0_kernel,
        out_shape=outs,
        grid=(B, N_TILES),
        in_specs=[
            _row_spec(D_MODEL), _mod_spec(),
            _const_spec(w_in.shape), _const_spec(g_q.shape), _const_spec(w_uq.shape),
            _const_spec(g_kv.shape), _const_spec(w_uk.shape), _const_spec(w_uv.shape),
            _table_spec(), _table_spec(), _table_spec(),
        ],
        out_specs=(_row_spec(HEADS_W), _row_spec(HEADS_W), _row_spec(HEADS_W), _row_spec(POOL_WIDTH)),
        compiler_params=_params(2),
        name="mla_pool_in_proj",
    )(xa, mods, w_in, g_q, w_uq, g_kv, w_uk, w_uv, *tabs)


def _in1_kernel(x_ref, mod_ref, w_in_ref, gq_ref, gk_ref, c_ref, s_ref, q_ref, k_ref, v_ref):
    m = mod_ref[0, 0]
    h = _bf(_rms(x_ref[0]) * (1.0 + m[1:2]) + m[0:1])
    a = _dot(h, w_in_ref[...])
    c, s = c_ref[...], s_ref[...]
    scale = GQA_HEAD_DIM ** -0.5

    def head(j, gain):
        xn = _rms(a[:, j * LANES:(j + 1) * LANES]) * gain
        return xn * c + pltpu.roll(xn, LANES // 2, axis=1) * s

    for j in range(GQA_HEADS):
        q_ref[0, :, j * LANES:(j + 1) * LANES] = _bf(head(j, gq_ref[...]) * scale)
    for j in range(GQA_KV_HEADS):
        k_ref[0, :, j * LANES:(j + 1) * LANES] = _bf(head(GQA_HEADS + j, gk_ref[...]))
    v_ref[0] = _bf(a[:, (GQA_HEADS + GQA_KV_HEADS) * LANES:])


def _in1(xa, mods, w_in, g_q, g_k, tabs):
    B = xa.shape[0]
    kv_w = GQA_KV_HEADS * LANES
    outs = (
        jax.ShapeDtypeStruct((B, T_ALL, HEADS_W), jnp.bfloat16),
        jax.ShapeDtypeStruct((B, T_ALL, kv_w), jnp.bfloat16),
        jax.ShapeDtypeStruct((B, T_ALL, kv_w), jnp.bfloat16),
    )
    return pl.pallas_call(
        _in1_kernel,
        out_shape=outs,
        grid=(B, N_TILES),
        in_specs=[
            _row_spec(D_MODEL), _mod_spec(), _const_spec(w_in.shape),
            _const_spec(g_q.shape), _const_spec(g_k.shape), _table_spec(), _table_spec(),
        ],
        out_specs=(_row_spec(HEADS_W), _row_spec(kv_w), _row_spec(kv_w)),
        compiler_params=_params(2),
        name="gqa_in_proj",
    )(xa, mods, w_in, g_q, g_k, *tabs)


def _softmax_step(q, k, v, m, l, acc):
    s = lax.dot_general(q, k, (((1,), (1,)), ((), ())), preferred_element_type=jnp.float32)
    m_new = jnp.maximum(m, jnp.max(s, axis=-1, keepdims=True))
    alpha = jnp.exp(m - m_new)
    p = jnp.exp(s - m_new)
    l = alpha * l + jnp.sum(p, axis=-1, keepdims=True)
    acc = alpha * acc + _dot(_bf(p), v)
    return m_new, l, acc


def _attn_kernel(q_ref, k_ref, v_ref, o_ref, *, heads_per_group, with_ctx):
    qi = pl.program_id(2)
    rows = q_ref.shape[1]

    def init():
        return (jnp.full((rows, 1), -jnp.inf, jnp.float32), jnp.zeros((rows, 1), jnp.float32),
                jnp.zeros((rows, LANES), jnp.float32))

    def finish(hd, state):
        _, l, acc = state
        o_ref[0, :, hd * LANES:(hd + 1) * LANES] = _bf(acc / l)

    def latent_rows():
        for hd in range(heads_per_group):
            q = q_ref[0, :, hd * LANES:(hd + 1) * LANES]

            def body(j, state):
                start = pl.multiple_of(j * KV_CHUNK, KV_CHUNK)
                return _softmax_step(q, k_ref[0, pl.ds(start, KV_CHUNK), :],
                                     v_ref[0, pl.ds(start, KV_CHUNK), :], *state)

            finish(hd, lax.fori_loop(0, T_ALL // KV_CHUNK, body, init()))

    def context_rows():
        for hd in range(heads_per_group):
            q = q_ref[0, :, hd * LANES:(hd + 1) * LANES]
            finish(hd, _softmax_step(q, k_ref[0, SEQ:, :], v_ref[0, SEQ:, :], *init()))

    if with_ctx:
        pl.when(qi < N_LAT_TILES)(latent_rows)
        pl.when(qi == N_LAT_TILES)(context_rows)
    else:
        latent_rows()


def _attention(q, k, v, *, groups, with_ctx):
    B = q.shape[0]
    hpg = q.shape[2] // (groups * LANES)
    n_q = N_TILES if with_ctx else N_LAT_TILES
    kv_spec = pl.BlockSpec((1, T_ALL, LANES), lambda b, g, i: (b, 0, g))
    q_spec = pl.BlockSpec((1, ROW_TILE, hpg * LANES), lambda b, g, i: (b, i, g))
    return pl.pallas_call(
        functools.partial(_attn_kernel, heads_per_group=hpg, with_ctx=with_ctx),
        out_shape=jax.ShapeDtypeStruct((B, n_q * ROW_TILE, q.shape[2]), jnp.bfloat16),
        grid=(B, groups, n_q),
        in_specs=[q_spec, kv_spec, kv_spec],
        out_specs=q_spec,
        compiler_params=_params(3),
        name="attention",
    )(q, k, v)


def _out0_kernel(x_ref, mod_ref, o_ref, pp_ref, p_ref, pn_ref, w_pool_ref, s_pool_ref,
                 w_oa_ref, w_op_ref, y_ref):
    t = pl.program_id(1)
    pe, prev_ok, next_ok = _with_halo(pp_ref, p_ref, pn_ref, t)
    pe = jnp.where(_halo_row_mask(prev_ok, next_ok), pe, 0.0)
    t0 = jnp.where(t == N_LAT_TILES, 0, t * ROW_TILE)
    t_len = jnp.where(t == N_LAT_TILES, CTX_LEN, SEQ)
    pos = t0 + lax.broadcasted_iota(jnp.int32, (ROW_TILE, 1), 0)

    pooled = []
    for g, w in enumerate(POOL_WINDOWS):
        run = pe[:, g * LANES:(g + 1) * LANES]
        span = 1
        while span < w:
            run = run + pltpu.roll(run, run.shape[0] - span, axis=0)
            span *= 2
        win = _shift_rows(run, w // 2)
        lo = jnp.clip(pos - w // 2, 0, t_len)
        hi = jnp.clip(pos - w // 2 + w, 0, t_len)
        d = win / (hi - lo).astype(jnp.float32) - p_ref[0, :, g * LANES:(g + 1) * LANES]
        pooled.append(_dot(_bf(d), w_pool_ref[g]))
    pooled = jnp.concatenate(pooled, axis=-1) * s_pool_ref[...]

    y = _dot(o_ref[0], w_oa_ref[...]) + _dot(_bf(pooled), w_op_ref[...])
    y_ref[0] = x_ref[0] + mod_ref[0, 0][2:3] * y


def _out0(xa, mods, o, p, w_pool, s_pool, w_oa, w_op):
    B = xa.shape[0]
    return pl.pallas_call(
        _out0_kernel,
        out_shape=jax.ShapeDtypeStruct(xa.shape, jnp.float32),
        grid=(B, N_TILES),
        in_specs=[
            _row_spec(D_MODEL), _mod_spec(), _row_spec(HEADS_W),
            _prev_halo_spec(POOL_WIDTH), _row_spec(POOL_WIDTH), _next_halo_spec(POOL_WIDTH),
            _const_spec(w_pool.shape), _const_spec(s_pool.shape),
            _const_spec(w_oa.shape), _const_spec(w_op.shape),
        ],
        out_specs=_row_spec(D_MODEL),
        compiler_params=_params(2),
        name="mla_pool_out_proj",
    )(xa, mods, o, p, p, p, w_pool, s_pool, w_oa, w_op)


def _out1_kernel(x_ref, mod_ref, o_ref, w_ref, y_ref):
    y_ref[0] = x_ref[0] + mod_ref[0, 0][2:3] * _dot(o_ref[0], w_ref[...])


def _out1(xa, mods, o, w_out):
    B = xa.shape[0]
    return pl.pallas_call(
        _out1_kernel,
        out_shape=jax.ShapeDtypeStruct((B, SEQ, D_MODEL), jnp.float32),
        grid=(B, N_LAT_TILES),
        in_specs=[_row_spec(D_MODEL), _mod_spec(), _row_spec(HEADS_W), _const_spec(w_out.shape)],
        out_specs=_row_spec(D_MODEL),
        compiler_params=_params(2),
        name="gqa_out_proj",
    )(xa, mods, o, w_out)


def _ffn_kernel(xp_ref, x_ref, xn_ref, mod_ref, w_up_ref, cw_ref, cb_ref, w_dn_ref, gf_ref, y_ref,
                *, final_norm):
    t = pl.program_id(1)
    m = mod_ref[0, 0]
    xe, prev_ok, next_ok = _with_halo(xp_ref, x_ref, xn_ref, t)
    he = _rms(xe) * (1.0 + m[4:5]) + m[3:4]
    he = _bf(jnp.where(_halo_row_mask(prev_ok, next_ok), he, 0.0))
    hm = he[SUBLANES:SUBLANES + ROW_TILE]

    y = jnp.zeros((ROW_TILE, D_MODEL), jnp.float32)
    for c0 in range(0, D_FF, FF_CHUNK):
        cs = slice(c0, c0 + FF_CHUNK)
        g = _dot(he, w_up_ref[:, cs])
        u = _dot(hm, w_up_ref[:, D_FF + c0:D_FF + c0 + FF_CHUNK])
        acc = cb_ref[:, cs] + _shift_rows(g, 1) * cw_ref[0:1, cs]
        acc = acc + g[SUBLANES:SUBLANES + ROW_TILE] * cw_ref[1:2, cs]
        acc = acc + _shift_rows(g, -1) * cw_ref[2:3, cs]
        y = y + _dot(_bf(jax.nn.silu(acc) * u), w_dn_ref[cs, :])
    out = x_ref[0] + m[5:6] * y
    if final_norm:
        out = _rms(out) * gf_ref[...]
    y_ref[0] = out


def _ffn(xa, mods, w_up, conv_w, conv_b, w_dn, g_final, *, final_norm):
    B = xa.shape[0]
    rows = xa.shape[1]
    n_t = rows // ROW_TILE
    assert final_norm == (rows == SEQ)
    return pl.pallas_call(
        functools.partial(_ffn_kernel, final_norm=final_norm),
        out_shape=jax.ShapeDtypeStruct((B, rows, D_MODEL), jnp.float32),
        grid=(B, n_t),
        in_specs=[
            _prev_halo_spec(D_MODEL), _row_spec(D_MODEL), _next_halo_spec(D_MODEL, rows), _mod_spec(),
            _const_spec(w_up.shape), _const_spec(conv_w.shape), _const_spec(conv_b.shape),
            _const_spec(w_dn.shape), _const_spec(g_final.shape),
        ],
        out_specs=_row_spec(D_MODEL),
        compiler_params=_params(2),
        name="conv_ffn",
    )(xa, xa, xa, mods, w_up, conv_w, conv_b, w_dn, g_final)


def _rope_angles(rope_dim):
    rows = SEQ // GRID_W
    row = jnp.repeat(jnp.arange(rows, dtype=jnp.float32), GRID_W)
    col = jnp.tile(jnp.arange(GRID_W, dtype=jnp.float32), rows)
    n_freq = rope_dim // 4
    freq = ROPE_THETA ** (-jnp.arange(n_freq, dtype=jnp.float32) / n_freq)
    ang = jnp.concatenate([row[:, None] * freq, col[:, None] * freq], axis=-1)
    return jnp.concatenate([ang, jnp.zeros((CTX_LEN, rope_dim // 2), jnp.float32)], axis=0)


def _mla_tables():
    ang = _rope_angles(MLA_ROPE)
    cos, sin = jnp.cos(ang), jnp.sin(ang)
    half = MLA_ROPE // 2
    ones = jnp.ones((T_ALL, MLA_NOPE), jnp.float32)
    zn = jnp.zeros((T_ALL, MLA_NOPE), jnp.float32)
    zh = jnp.zeros((T_ALL, half), jnp.float32)
    zt = jnp.zeros((T_ALL, LANES - MLA_NOPE - MLA_ROPE), jnp.float32)
    c = jnp.concatenate([ones, cos, cos, zt], axis=-1)
    s_up = jnp.concatenate([zn, zh, sin, zt], axis=-1)
    s_dn = jnp.concatenate([zn, -sin, zh, zt], axis=-1)
    return c, s_up, s_dn


def _gqa_tables():
    ang = _rope_angles(GQA_HEAD_DIM)
    cos, sin = jnp.cos(ang), jnp.sin(ang)
    return jnp.concatenate([cos, cos], axis=-1), jnp.concatenate([-sin, sin], axis=-1)


def _mla_head_cols(w, n_heads, nope, rope):
    K = w.shape[0]
    w = w.reshape(K, n_heads, nope + rope)
    parts = [w[..., :nope]]
    if rope:
        parts += [w[..., nope::2], w[..., nope + 1::2]]
    parts.append(jnp.zeros((K, n_heads, LANES - nope - rope), w.dtype))
    return jnp.concatenate(parts, axis=-1).reshape(K, n_heads * LANES)


def _deinterleave(n):
    return np.concatenate([np.arange(0, n, 2), np.arange(1, n, 2)])


def kernel(x, c, ctx, c_ctx, w_mod, b_mod, mix0_w_in, mla_g_q, mla_w_uq, mla_g_kv, mla_w_uk, mla_w_uv,
           pool_w, pool_scale, mix0_w_out, gqa_w_in, gqa_g_q, gqa_g_k, gqa_w_out,
           ffn_w_up, ffn_conv_w, ffn_conv_b, ffn_w_down, g_final):
    B = x.shape[0]
    assert x.shape == (B, SEQ, D_MODEL) and ctx.shape == (B, CTX_LEN, D_MODEL) and B < SUBLANES
    assert w_mod.shape[0] == 2

    cc = jnp.zeros((SUBLANES, D_MODEL), jnp.float32).at[:B].set(c).at[B].set(c_ctx)
    mod_all = _modulation(cc, w_mod, b_mod).reshape(2, SUBLANES, N_MOD, D_MODEL)

    def mods_of(i):
        lat = mod_all[i, :B]
        con = jnp.broadcast_to(mod_all[i, B], lat.shape)
        return jnp.stack([lat, con], axis=1)

    xa = jnp.concatenate([x, ctx], axis=1)

    w_in = mix0_w_in[0]
    i_kv = MLA_Q_RANK
    i_kr = i_kv + MLA_KV_RANK
    i_p = i_kr + MLA_ROPE
    w_kr = _mla_head_cols(w_in[:, i_kr:i_p], 1, 0, MLA_ROPE)
    w_kr = jnp.roll(w_kr, MLA_NOPE, axis=1)
    w_in0 = _bf(jnp.concatenate([w_in[:, :i_kr], w_kr, w_in[:, i_p:]], axis=1))
    w_uq = _bf(_mla_head_cols(mla_w_uq[0], MLA_HEADS, MLA_NOPE, MLA_ROPE))
    w_uk = _bf(_mla_head_cols(mla_w_uk[0], MLA_HEADS, MLA_NOPE, 0))
    w_uv = _bf(_mla_head_cols(mla_w_uv[0], MLA_HEADS, MLA_V, 0))
    w_out0 = mix0_w_out[0]
    w_oa = w_out0[:MLA_HEADS * MLA_V].reshape(MLA_HEADS, MLA_V, D_MODEL)
    w_oa = _bf(jnp.pad(w_oa, ((0, 0), (0, LANES - MLA_V), (0, 0))).reshape(HEADS_W, D_MODEL))
    w_op = _bf(w_out0[MLA_HEADS * MLA_V:])

    m0 = mods_of(0)
    q, k, v, p = _in0(xa, m0, w_in0, mla_g_q[0][None], w_uq, mla_g_kv[0][None], w_uk, w_uv, _mla_tables())
    o = _attention(q, k, v, groups=MLA_HEADS, with_ctx=True)
    xa = _out0(xa, m0, o, p, _bf(pool_w[0]), pool_scale[0][None], w_oa, w_op)
    xa = _ffn(xa, m0, _bf(ffn_w_up[0]), ffn_conv_w[0], ffn_conv_b[0][None], _bf(ffn_w_down[0]),
              g_final[None], final_norm=False)

    perm = _deinterleave(GQA_HEAD_DIM)
    n_qk = GQA_HEADS + GQA_KV_HEADS
    w_in = gqa_w_in[0]
    w_qk = w_in[:, :n_qk * LANES].reshape(D_MODEL, n_qk, LANES)[:, :, perm].reshape(D_MODEL, n_qk * LANES)
    w_in1 = _bf(jnp.concatenate([w_qk, w_in[:, n_qk * LANES:]], axis=1))
    m1 = mods_of(1)
    q, k, v = _in1(xa, m1, w_in1, gqa_g_q[0][perm][None], gqa_g_k[0][perm][None], _gqa_tables())
    o = _attention(q, k, v, groups=GQA_KV_HEADS, with_ctx=False)
    xa = _out1(xa, m1, o, _bf(gqa_w_out[0]))
    return _ffn(xa, m1, _bf(ffn_w_up[1]), ffn_conv_w[1], ffn_conv_b[1][None], _bf(ffn_w_down[1]),
                g_final[None], final_norm=True)
```

```python
import functools

import jax
import jax.numpy as jnp
import numpy as np
from jax import lax
from jax.experimental import pallas as pl
from jax.experimental.pallas import tpu as pltpu

D_MODEL = 1024
SEQ = 8192
GRID_W = 64
CTX_LEN = 256
T_ALL = SEQ + CTX_LEN
ROPE_THETA = 10000.0
EPS = 1e-6
N_MOD = 6
MLA_HEADS = 8
MLA_NOPE = 64
MLA_ROPE = 32
MLA_V = 64
MLA_Q_RANK = 384
MLA_KV_RANK = 256
POOL_WINDOWS = (2, 4, 8, 16)
POOL_WIDTH = 512
GQA_HEADS = 8
GQA_KV_HEADS = 2
GQA_HEAD_DIM = 128
GQA_GROUP = GQA_HEADS // GQA_KV_HEADS
D_FF = 2816
CONV_W = 3

LANES = 128
SUBLANES = 8
ROW_TILE = 256
N_LAT_TILES = SEQ // ROW_TILE
N_TILES = T_ALL // ROW_TILE
HALO_BLOCKS_PER_TILE = ROW_TILE // SUBLANES
KV_CHUNK = 768
FF_CHUNK = 256
VMEM_LIMIT = 56 * 1024 * 1024

MIX0_IN_PAD = MLA_Q_RANK + MLA_KV_RANK + LANES + POOL_WIDTH
HEADS_W = MLA_HEADS * LANES
LOG2_E = 1.4426950408889634
ONES_ROWS = 16
MLA_VROWS = MLA_V + ONES_ROWS
GQA_VROWS = GQA_HEAD_DIM + ONES_ROWS
MLA_HEADS_PER_STEP = 4


def _params(n_axes):
    return pltpu.CompilerParams(dimension_semantics=("arbitrary",) * n_axes, vmem_limit_bytes=VMEM_LIMIT)


def _rms(x):
    return x * lax.rsqrt(jnp.mean(x * x, axis=-1, keepdims=True) + EPS)


def _dot(a, b):
    return jnp.dot(a, b, preferred_element_type=jnp.float32)


def _dot_nt(a, b):
    return lax.dot_general(a, b, (((1,), (1,)), ((), ())), preferred_element_type=jnp.float32)


def _bf(x):
    return x.astype(jnp.bfloat16)


def _mod_kernel(c_ref, w_ref, b_ref, o_ref):
    s = jax.nn.silu(c_ref[...])
    o_ref[0] = jnp.dot(s, w_ref[0], preferred_element_type=jnp.float32,
                       precision=lax.Precision.HIGHEST) + b_ref[0]


def _modulation(cc, w_mod, b_mod):
    depth = w_mod.shape[0]
    return pl.pallas_call(
        _mod_kernel,
        out_shape=jax.ShapeDtypeStruct((depth, SUBLANES, N_MOD * D_MODEL), jnp.float32),
        grid=(depth, N_MOD),
        in_specs=[
            pl.BlockSpec((SUBLANES, D_MODEL), lambda i, j: (0, 0)),
            pl.BlockSpec((1, D_MODEL, D_MODEL), lambda i, j: (i, 0, j)),
            pl.BlockSpec((1, 1, D_MODEL), lambda i, j: (i, 0, j)),
        ],
        out_specs=pl.BlockSpec((1, SUBLANES, D_MODEL), lambda i, j: (i, 0, j)),
        compiler_params=_params(2),
        name="modulation",
    )(cc, w_mod, b_mod.reshape(depth, 1, N_MOD * D_MODEL))


def _row_spec(width, rows=ROW_TILE):
    return pl.BlockSpec((1, rows, width), lambda b, t: (b, t, 0))


def _mod_spec():
    return pl.BlockSpec((1, 1, N_MOD, D_MODEL), lambda b, t: (b, t // N_LAT_TILES, 0, 0))


def _const_spec(shape):
    nd = len(shape)
    return pl.BlockSpec(shape, lambda b, t: (0,) * nd)


def _table_spec():
    return pl.BlockSpec((ROW_TILE, LANES), lambda b, t: (t, 0))


def _prev_halo_spec(width):
    return pl.BlockSpec((1, SUBLANES, width),
                        lambda b, t: (b, jnp.maximum(t * HALO_BLOCKS_PER_TILE - 1, 0), 0))


def _next_halo_spec(width, total_rows=T_ALL):
    last = total_rows // SUBLANES - 1
    return pl.BlockSpec((1, SUBLANES, width),
                        lambda b, t: (b, jnp.minimum((t + 1) * HALO_BLOCKS_PER_TILE, last), 0))


def _halo_valid(t):
    prev_ok = jnp.logical_and(t != 0, t != N_LAT_TILES)
    next_ok = jnp.logical_and(t != N_LAT_TILES - 1, t != N_LAT_TILES)
    return prev_ok, next_ok


def _with_halo(prev_ref, main_ref, next_ref, t):
    prev_ok, next_ok = _halo_valid(t)
    return jnp.concatenate([prev_ref[0], main_ref[0], next_ref[0]], axis=0), prev_ok, next_ok


def _halo_row_mask(prev_ok, next_ok):
    r = lax.broadcasted_iota(jnp.int32, (ROW_TILE + 2 * SUBLANES, 1), 0)
    return jnp.logical_and(jnp.logical_or(r >= SUBLANES, prev_ok),
                           jnp.logical_or(r < ROW_TILE + SUBLANES, next_ok))


def _shift_rows(x, k):
    n = x.shape[0]
    return pltpu.roll(x, k % n, axis=0)[SUBLANES:SUBLANES + ROW_TILE]


def _rope_mla(x, c, s_up, s_dn):
    return x * c + pltpu.roll(x, MLA_ROPE // 2, axis=1) * s_up + pltpu.roll(x, LANES - MLA_ROPE // 2, axis=1) * s_dn


def _in0_kernel(x_ref, mod_ref, w_in_ref, gq_ref, w_uq_ref, gkv_ref, w_uk_ref, w_uvt_ref,
                c_ref, su_ref, sd_ref, q_ref, k_ref, v_ref, p_ref):
    m = mod_ref[0, 0]
    h = _bf(_rms(x_ref[0]) * (1.0 + m[1:2]) + m[0:1])
    a = _dot(h, w_in_ref[...])
    i_kv = MLA_Q_RANK
    i_kr = i_kv + MLA_KV_RANK
    i_p = i_kr + LANES
    c, su, sd = c_ref[...], su_ref[...], sd_ref[...]
    scale = (MLA_NOPE + MLA_ROPE) ** -0.5 * LOG2_E

    q = _dot(_bf(_rms(a[:, :i_kv]) * gq_ref[...]), w_uq_ref[...])
    ckv = _bf(_rms(a[:, i_kv:i_kr]) * gkv_ref[...])
    kn = _dot(ckv, w_uk_ref[...])
    kr = _rope_mla(a[:, i_kr:i_p], c, su, sd)
    vt = _dot_nt(w_uvt_ref[...], ckv)
    ones = jnp.ones((ONES_ROWS, vt.shape[1]), jnp.bfloat16)
    for hd in range(MLA_HEADS):
        sl = slice(hd * LANES, (hd + 1) * LANES)
        q_ref[0, :, sl] = _bf(_rope_mla(q[:, sl], c, su, sd) * scale)
        k_ref[0, :, sl] = _bf(kn[:, sl] + kr)
        v_ref[0, hd * MLA_VROWS:hd * MLA_VROWS + MLA_V, :] = _bf(vt[hd * MLA_V:(hd + 1) * MLA_V])
        v_ref[0, hd * MLA_VROWS + MLA_V:(hd + 1) * MLA_VROWS, :] = ones
    p_ref[0] = a[:, i_p:]


def _vt_spec(rows):
    return pl.BlockSpec((1, rows, ROW_TILE), lambda b, t: (b, 0, t))


def _in0(xa, mods, w_in, g_q, w_uq, g_kv, w_uk, w_uvt, tabs):
    B = xa.shape[0]
    outs = (
        jax.ShapeDtypeStruct((B, T_ALL, HEADS_W), jnp.bfloat16),
        jax.ShapeDtypeStruct((B, T_ALL, HEADS_W), jnp.bfloat16),
        jax.ShapeDtypeStruct((B, MLA_HEADS * MLA_VROWS, T_ALL), jnp.bfloat16),
        jax.ShapeDtypeStruct((B, T_ALL, POOL_WIDTH), jnp.float32),
    )
    return pl.pallas_call(
        _in0_kernel,
        out_shape=outs,
        grid=(B, N_TILES),
        in_specs=[
            _row_spec(D_MODEL), _mod_spec(),
            _const_spec(w_in.shape), _const_spec(g_q.shape), _const_spec(w_uq.shape),
            _const_spec(g_kv.shape), _const_spec(w_uk.shape), _const_spec(w_uvt.shape),
            _table_spec(), _table_spec(), _table_spec(),
        ],
        out_specs=(_row_spec(HEADS_W), _row_spec(HEADS_W), _vt_spec(MLA_HEADS * MLA_VROWS),
                   _row_spec(POOL_WIDTH)),
        compiler_params=_params(2),
        name="mla_pool_in_proj",
    )(xa, mods, w_in, g_q, w_uq, g_kv, w_uk, w_uvt, *tabs)


def _in1_kernel(x_ref, mod_ref, w_in_ref, w_vt_ref, gq_ref, gk_ref, c_ref, s_ref, q_ref, k_ref, v_ref):
    m = mod_ref[0, 0]
    h = _bf(_rms(x_ref[0]) * (1.0 + m[1:2]) + m[0:1])
    a = _dot(h, w_in_ref[...])
    c, s = c_ref[...], s_ref[...]
    scale = GQA_HEAD_DIM ** -0.5 * LOG2_E

    def head(j, gain):
        xn = _rms(a[:, j * LANES:(j + 1) * LANES]) * gain
        return xn * c + pltpu.roll(xn, LANES // 2, axis=1) * s

    for j in range(GQA_HEADS):
        q_ref[0, :, j * LANES:(j + 1) * LANES] = _bf(head(j, gq_ref[...]) * scale)
    for j in range(GQA_KV_HEADS):
        k_ref[0, :, j * LANES:(j + 1) * LANES] = _bf(head(GQA_HEADS + j, gk_ref[...]))
    vt = _dot_nt(w_vt_ref[...], h)
    ones = jnp.ones((ONES_ROWS, vt.shape[1]), jnp.bfloat16)
    for j in range(GQA_KV_HEADS):
        v_ref[0, j * GQA_VROWS:j * GQA_VROWS + GQA_HEAD_DIM, :] = _bf(vt[j * GQA_HEAD_DIM:(j + 1) * GQA_HEAD_DIM])
        v_ref[0, j * GQA_VROWS + GQA_HEAD_DIM:(j + 1) * GQA_VROWS, :] = ones


def _in1(xa, mods, w_in, w_vt, g_q, g_k, tabs):
    B = xa.shape[0]
    kv_w = GQA_KV_HEADS * LANES
    outs = (
        jax.ShapeDtypeStruct((B, T_ALL, HEADS_W), jnp.bfloat16),
        jax.ShapeDtypeStruct((B, T_ALL, kv_w), jnp.bfloat16),
        jax.ShapeDtypeStruct((B, GQA_KV_HEADS * GQA_VROWS, T_ALL), jnp.bfloat16),
    )
    return pl.pallas_call(
        _in1_kernel,
        out_shape=outs,
        grid=(B, N_TILES),
        in_specs=[
            _row_spec(D_MODEL), _mod_spec(), _const_spec(w_in.shape), _const_spec(w_vt.shape),
            _const_spec(g_q.shape), _const_spec(g_k.shape), _table_spec(), _table_spec(),
        ],
        out_specs=(_row_spec(HEADS_W), _row_spec(kv_w), _vt_spec(GQA_KV_HEADS * GQA_VROWS)),
        compiler_params=_params(2),
        name="gqa_in_proj",
    )(xa, mods, w_in, w_vt, g_q, g_k, *tabs)


def _softmax_steps(qs, ks, vts, states):
    ss = [_dot_nt(k, q) for q, k in zip(qs, ks)]
    m_new = [jnp.maximum(m, jnp.max(s, axis=0, keepdims=True)) for s, (m, _) in zip(ss, states)]
    ps = [_bf(jnp.exp2(s - mn)) for s, mn in zip(ss, m_new)]
    accs = [jnp.exp2(m - mn) * acc + _dot(vt, p) for (m, acc), mn, vt, p in zip(states, m_new, vts, ps)]
    return tuple(zip(m_new, accs))


def _attn_kernel(q_ref, k_ref, vt_ref, o_ref, *, n_heads, shared_kv, dv, with_ctx):
    qi = pl.program_id(2)
    tq = q_ref.shape[1]
    vrows = dv + ONES_ROWS

    def q_of(hd):
        return q_ref[0, :, hd * LANES:(hd + 1) * LANES]

    def kv_of(hd, rows):
        j = 0 if shared_kv else hd
        return (k_ref[0, rows, j * LANES:(j + 1) * LANES], vt_ref[0, j * vrows:(j + 1) * vrows, rows])

    def init():
        return jnp.full((1, tq), -jnp.inf, jnp.float32), jnp.zeros((vrows, tq), jnp.float32)

    def step(qs, rows, states):
        kvs = [kv_of(hd, rows) for hd in range(n_heads)]
        return _softmax_steps(qs, [k for k, _ in kvs], [vt for _, vt in kvs], states)

    def finish(states):
        ot = jnp.concatenate([acc[:dv] / acc[dv:dv + 1] for _, acc in states], axis=0)
        o_ref[0] = _bf(ot.T)

    def latent_rows():
        qs = [q_of(hd) for hd in range(n_heads)]

        def body(j, states):
            return step(qs, pl.ds(pl.multiple_of(j * KV_CHUNK, KV_CHUNK), KV_CHUNK), states)

        finish(lax.fori_loop(0, T_ALL // KV_CHUNK, body, tuple(init() for _ in range(n_heads))))

    def context_rows():
        qs = [q_of(hd) for hd in range(n_heads)]
        finish(step(qs, pl.ds(SEQ, CTX_LEN), tuple(init() for _ in range(n_heads))))

    if with_ctx:
        pl.when(qi < N_LAT_TILES)(latent_rows)
        pl.when(qi == N_LAT_TILES)(context_rows)
    else:
        latent_rows()


def _attention(q, k, vt, *, n_heads, shared_kv, dv, with_ctx):
    B = q.shape[0]
    groups = q.shape[2] // (n_heads * LANES)
    kv_heads = 1 if shared_kv else n_heads
    vrows = dv + ONES_ROWS
    n_q = N_TILES if with_ctx else N_LAT_TILES
    q_spec = pl.BlockSpec((1, ROW_TILE, n_heads * LANES), lambda b, g, i: (b, i, g))
    k_spec = pl.BlockSpec((1, T_ALL, kv_heads * LANES), lambda b, g, i: (b, 0, g))
    vt_spec = pl.BlockSpec((1, kv_heads * vrows, T_ALL), lambda b, g, i: (b, g, 0))
    o_spec = pl.BlockSpec((1, ROW_TILE, n_heads * dv), lambda b, g, i: (b, i, g))
    return pl.pallas_call(
        functools.partial(_attn_kernel, n_heads=n_heads, shared_kv=shared_kv, dv=dv, with_ctx=with_ctx),
        out_shape=jax.ShapeDtypeStruct((B, n_q * ROW_TILE, groups * n_heads * dv), jnp.bfloat16),
        grid=(B, groups, n_q),
        in_specs=[q_spec, k_spec, vt_spec],
        out_specs=o_spec,
        compiler_params=_params(3),
        name="attention",
    )(q, k, vt)


def _out0_kernel(x_ref, mod_ref, o_ref, pp_ref, p_ref, pn_ref, w_pool_ref, s_pool_ref,
                 w_oa_ref, w_op_ref, y_ref):
    t = pl.program_id(1)
    pe, prev_ok, next_ok = _with_halo(pp_ref, p_ref, pn_ref, t)
    pe = jnp.where(_halo_row_mask(prev_ok, next_ok), pe, 0.0)
    t0 = jnp.where(t == N_LAT_TILES, 0, t * ROW_TILE)
    t_len = jnp.where(t == N_LAT_TILES, CTX_LEN, SEQ)
    pos = t0 + lax.broadcasted_iota(jnp.int32, (ROW_TILE, 1), 0)

    pooled = []
    for g, w in enumerate(POOL_WINDOWS):
        run = pe[:, g * LANES:(g + 1) * LANES]
        span = 1
        while span < w:
            run = run + pltpu.roll(run, run.shape[0] - span, axis=0)
            span *= 2
        win = _shift_rows(run, w // 2)
        lo = jnp.clip(pos - w // 2, 0, t_len)
        hi = jnp.clip(pos - w // 2 + w, 0, t_len)
        d = win / (hi - lo).astype(jnp.float32) - p_ref[0, :, g * LANES:(g + 1) * LANES]
        pooled.append(_dot(_bf(d), w_pool_ref[g]))
    pooled = jnp.concatenate(pooled, axis=-1) * s_pool_ref[...]

    y = _dot(o_ref[0], w_oa_ref[...]) + _dot(_bf(pooled), w_op_ref[...])
    y_ref[0] = x_ref[0] + mod_ref[0, 0][2:3] * y


def _out0(xa, mods, o, p, w_pool, s_pool, w_oa, w_op):
    B = xa.shape[0]
    return pl.pallas_call(
        _out0_kernel,
        out_shape=jax.ShapeDtypeStruct(xa.shape, jnp.float32),
        grid=(B, N_TILES),
        in_specs=[
            _row_spec(D_MODEL), _mod_spec(), _row_spec(o.shape[2]),
            _prev_halo_spec(POOL_WIDTH), _row_spec(POOL_WIDTH), _next_halo_spec(POOL_WIDTH),
            _const_spec(w_pool.shape), _const_spec(s_pool.shape),
            _const_spec(w_oa.shape), _const_spec(w_op.shape),
        ],
        out_specs=_row_spec(D_MODEL),
        compiler_params=_params(2),
        name="mla_pool_out_proj",
    )(xa, mods, o, p, p, p, w_pool, s_pool, w_oa, w_op)


def _out1_kernel(x_ref, mod_ref, o_ref, w_ref, y_ref):
    y_ref[0] = x_ref[0] + mod_ref[0, 0][2:3] * _dot(o_ref[0], w_ref[...])


def _out1(xa, mods, o, w_out):
    B = xa.shape[0]
    return pl.pallas_call(
        _out1_kernel,
        out_shape=jax.ShapeDtypeStruct((B, SEQ, D_MODEL), jnp.float32),
        grid=(B, N_LAT_TILES),
        in_specs=[_row_spec(D_MODEL), _mod_spec(), _row_spec(HEADS_W), _const_spec(w_out.shape)],
        out_specs=_row_spec(D_MODEL),
        compiler_params=_params(2),
        name="gqa_out_proj",
    )(xa, mods, o, w_out)


def _ffn_kernel(xp_ref, x_ref, xn_ref, mod_ref, w_up_ref, cw_ref, cb_ref, w_dn_ref, gf_ref, y_ref,
                *, final_norm):
    t = pl.program_id(1)
    m = mod_ref[0, 0]
    xe, prev_ok, next_ok = _with_halo(xp_ref, x_ref, xn_ref, t)
    he = _rms(xe) * (1.0 + m[4:5]) + m[3:4]
    he = _bf(jnp.where(_halo_row_mask(prev_ok, next_ok), he, 0.0))
    hm = he[SUBLANES:SUBLANES + ROW_TILE]

    y = jnp.zeros((ROW_TILE, D_MODEL), jnp.float32)
    for c0 in range(0, D_FF, FF_CHUNK):
        cs = slice(c0, c0 + FF_CHUNK)
        g = _dot(he, w_up_ref[:, cs])
        u = _dot(hm, w_up_ref[:, D_FF + c0:D_FF + c0 + FF_CHUNK])
        acc = cb_ref[:, cs] + _shift_rows(g, 1) * cw_ref[0:1, cs]
        acc = acc + g[SUBLANES:SUBLANES + ROW_TILE] * cw_ref[1:2, cs]
        acc = acc + _shift_rows(g, -1) * cw_ref[2:3, cs]
        y = y + _dot(_bf(jax.nn.silu(acc) * u), w_dn_ref[cs, :])
    out = x_ref[0] + m[5:6] * y
    if final_norm:
        out = _rms(out) * gf_ref[...]
    y_ref[0] = out


def _ffn(xa, mods, w_up, conv_w, conv_b, w_dn, g_final, *, final_norm):
    B = xa.shape[0]
    rows = xa.shape[1]
    n_t = rows // ROW_TILE
    assert final_norm == (rows == SEQ)
    return pl.pallas_call(
        functools.partial(_ffn_kernel, final_norm=final_norm),
        out_shape=jax.ShapeDtypeStruct((B, rows, D_MODEL), jnp.float32),
        grid=(B, n_t),
        in_specs=[
            _prev_halo_spec(D_MODEL), _row_spec(D_MODEL), _next_halo_spec(D_MODEL, rows), _mod_spec(),
            _const_spec(w_up.shape), _const_spec(conv_w.shape), _const_spec(conv_b.shape),
            _const_spec(w_dn.shape), _const_spec(g_final.shape),
        ],
        out_specs=_row_spec(D_MODEL),
        compiler_params=_params(2),
        name="conv_ffn",
    )(xa, xa, xa, mods, w_up, conv_w, conv_b, w_dn, g_final)


def _rope_angles(rope_dim):
    rows = SEQ // GRID_W
    row = jnp.repeat(jnp.arange(rows, dtype=jnp.float32), GRID_W)
    col = jnp.tile(jnp.arange(GRID_W, dtype=jnp.float32), rows)
    n_freq = rope_dim // 4
    freq = ROPE_THETA ** (-jnp.arange(n_freq, dtype=jnp.float32) / n_freq)
    ang = jnp.concatenate([row[:, None] * freq, col[:, None] * freq], axis=-1)
    return jnp.concatenate([ang, jnp.zeros((CTX_LEN, rope_dim // 2), jnp.float32)], axis=0)


def _mla_tables():
    ang = _rope_angles(MLA_ROPE)
    cos, sin = jnp.cos(ang), jnp.sin(ang)
    half = MLA_ROPE // 2
    ones = jnp.ones((T_ALL, MLA_NOPE), jnp.float32)
    zn = jnp.zeros((T_ALL, MLA_NOPE), jnp.float32)
    zh = jnp.zeros((T_ALL, half), jnp.float32)
    zt = jnp.zeros((T_ALL, LANES - MLA_NOPE - MLA_ROPE), jnp.float32)
    c = jnp.concatenate([ones, cos, cos, zt], axis=-1)
    s_up = jnp.concatenate([zn, zh, sin, zt], axis=-1)
    s_dn = jnp.concatenate([zn, -sin, zh, zt], axis=-1)
    return c, s_up, s_dn


def _gqa_tables():
    ang = _rope_angles(GQA_HEAD_DIM)
    cos, sin = jnp.cos(ang), jnp.sin(ang)
    return jnp.concatenate([cos, cos], axis=-1), jnp.concatenate([-sin, sin], axis=-1)


def _mla_head_cols(w, n_heads, nope, rope):
    K = w.shape[0]
    w = w.reshape(K, n_heads, nope + rope)
    parts = [w[..., :nope]]
    if rope:
        parts += [w[..., nope::2], w[..., nope + 1::2]]
    parts.append(jnp.zeros((K, n_heads, LANES - nope - rope), w.dtype))
    return jnp.concatenate(parts, axis=-1).reshape(K, n_heads * LANES)


def _deinterleave(n):
    return np.concatenate([np.arange(0, n, 2), np.arange(1, n, 2)])


def kernel(x, c, ctx, c_ctx, w_mod, b_mod, mix0_w_in, mla_g_q, mla_w_uq, mla_g_kv, mla_w_uk, mla_w_uv,
           pool_w, pool_scale, mix0_w_out, gqa_w_in, gqa_g_q, gqa_g_k, gqa_w_out,
           ffn_w_up, ffn_conv_w, ffn_conv_b, ffn_w_down, g_final):
    B = x.shape[0]
    assert x.shape == (B, SEQ, D_MODEL) and ctx.shape == (B, CTX_LEN, D_MODEL) and B < SUBLANES
    assert w_mod.shape[0] == 2

    cc = jnp.zeros((SUBLANES, D_MODEL), jnp.float32).at[:B].set(c).at[B].set(c_ctx)
    mod_all = _modulation(cc, w_mod, b_mod).reshape(2, SUBLANES, N_MOD, D_MODEL)

    def mods_of(i):
        lat = mod_all[i, :B]
        con = jnp.broadcast_to(mod_all[i, B], lat.shape)
        return jnp.stack([lat, con], axis=1)

    xa = jnp.concatenate([x, ctx], axis=1)

    w_in = mix0_w_in[0]
    i_kv = MLA_Q_RANK
    i_kr = i_kv + MLA_KV_RANK
    i_p = i_kr + MLA_ROPE
    w_kr = _mla_head_cols(w_in[:, i_kr:i_p], 1, 0, MLA_ROPE)
    w_kr = jnp.roll(w_kr, MLA_NOPE, axis=1)
    w_in0 = _bf(jnp.concatenate([w_in[:, :i_kr], w_kr, w_in[:, i_p:]], axis=1))
    w_uq = _bf(_mla_head_cols(mla_w_uq[0], MLA_HEADS, MLA_NOPE, MLA_ROPE))
    w_uk = _bf(_mla_head_cols(mla_w_uk[0], MLA_HEADS, MLA_NOPE, 0))
    w_uvt = _bf(mla_w_uv[0].T)
    w_out0 = mix0_w_out[0]
    w_oa = _bf(w_out0[:MLA_HEADS * MLA_V])
    w_op = _bf(w_out0[MLA_HEADS * MLA_V:])

    m0 = mods_of(0)
    q, k, vt, p = _in0(xa, m0, w_in0, mla_g_q[0][None], w_uq, mla_g_kv[0][None], w_uk, w_uvt, _mla_tables())
    o = _attention(q, k, vt, n_heads=MLA_HEADS_PER_STEP, shared_kv=False, dv=MLA_V, with_ctx=True)
    xa = _out0(xa, m0, o, p, _bf(pool_w[0]), pool_scale[0][None], w_oa, w_op)
    xa = _ffn(xa, m0, _bf(ffn_w_up[0]), ffn_conv_w[0], ffn_conv_b[0][None], _bf(ffn_w_down[0]),
              g_final[None], final_norm=False)

    perm = _deinterleave(GQA_HEAD_DIM)
    n_qk = GQA_HEADS + GQA_KV_HEADS
    w_in = gqa_w_in[0]
    w_qk = w_in[:, :n_qk * LANES].reshape(D_MODEL, n_qk, LANES)[:, :, perm].reshape(D_MODEL, n_qk * LANES)
    w_vt = _bf(w_in[:, n_qk * LANES:].T)
    m1 = mods_of(1)
    q, k, vt = _in1(xa, m1, _bf(w_qk), w_vt, gqa_g_q[0][perm][None], gqa_g_k[0][perm][None], _gqa_tables())
    o = _attention(q, k, vt, n_heads=GQA_GROUP, shared_kv=True, dv=GQA_HEAD_DIM, with_ctx=False)
    xa = _out1(xa, m1, o, _bf(gqa_w_out[0]))
    return _ffn(xa, m1, _bf(ffn_w_up[1]), ffn_conv_w[1], ffn_conv_b[1][None], _bf(ffn_w_down[1]),
                g_final[None], final_norm=True)
```

```python
import functools

import jax
import jax.numpy as jnp
import numpy as np
from jax import lax
from jax.experimental import pallas as pl
from jax.experimental.pallas import tpu as pltpu

D_MODEL = 1024
SEQ = 8192
GRID_W = 64
CTX_LEN = 256
T_ALL = SEQ + CTX_LEN
ROPE_THETA = 10000.0
EPS = 1e-6
N_MOD = 6
MLA_HEADS = 8
MLA_NOPE = 64
MLA_ROPE = 32
MLA_V = 64
MLA_Q_RANK = 384
MLA_KV_RANK = 256
POOL_WINDOWS = (2, 4, 8, 16)
POOL_WIDTH = 512
GQA_HEADS = 8
GQA_KV_HEADS = 2
GQA_HEAD_DIM = 128
GQA_GROUP = GQA_HEADS // GQA_KV_HEADS
D_FF = 2816
CONV_W = 3

LANES = 128
SUBLANES = 8
ROW_TILE = 256
N_LAT_TILES = SEQ // ROW_TILE
N_TILES = T_ALL // ROW_TILE
HALO_BLOCKS_PER_TILE = ROW_TILE // SUBLANES
KV_CHUNK = 768
FF_CHUNK = 256
VMEM_LIMIT = 56 * 1024 * 1024

MIX0_IN_PAD = MLA_Q_RANK + MLA_KV_RANK + LANES + POOL_WIDTH
HEADS_W = MLA_HEADS * LANES
LOG2_E = 1.4426950408889634
ONES_ROWS = 16
MLA_VROWS = MLA_V + ONES_ROWS
GQA_VROWS = GQA_HEAD_DIM + ONES_ROWS
MLA_HEADS_PER_STEP = 4


def _params(n_axes):
    return pltpu.CompilerParams(dimension_semantics=("arbitrary",) * n_axes, vmem_limit_bytes=VMEM_LIMIT)


def _rms(x):
    return x * lax.rsqrt(jnp.mean(x * x, axis=-1, keepdims=True) + EPS)


def _dot(a, b):
    return jnp.dot(a, b, preferred_element_type=jnp.float32)


def _dot_nt(a, b):
    return lax.dot_general(a, b, (((1,), (1,)), ((), ())), preferred_element_type=jnp.float32)


def _bf(x):
    return x.astype(jnp.bfloat16)


def _mod_kernel(c_ref, w_ref, b_ref, o_ref):
    s = jax.nn.silu(c_ref[...])
    o_ref[0] = jnp.dot(s, w_ref[0], preferred_element_type=jnp.float32,
                       precision=lax.Precision.HIGHEST) + b_ref[0]


def _modulation(cc, w_mod, b_mod):
    depth = w_mod.shape[0]
    return pl.pallas_call(
        _mod_kernel,
        out_shape=jax.ShapeDtypeStruct((depth, SUBLANES, N_MOD * D_MODEL), jnp.float32),
        grid=(depth, N_MOD),
        in_specs=[
            pl.BlockSpec((SUBLANES, D_MODEL), lambda i, j: (0, 0)),
            pl.BlockSpec((1, D_MODEL, D_MODEL), lambda i, j: (i, 0, j)),
            pl.BlockSpec((1, 1, D_MODEL), lambda i, j: (i, 0, j)),
        ],
        out_specs=pl.BlockSpec((1, SUBLANES, D_MODEL), lambda i, j: (i, 0, j)),
        compiler_params=_params(2),
        name="modulation",
    )(cc, w_mod, b_mod.reshape(depth, 1, N_MOD * D_MODEL))


def _row_spec(width, rows=ROW_TILE):
    return pl.BlockSpec((1, rows, width), lambda b, t: (b, t, 0))


def _mod_spec():
    return pl.BlockSpec((1, 1, N_MOD, D_MODEL), lambda b, t: (b, t // N_LAT_TILES, 0, 0))


def _const_spec(shape):
    nd = len(shape)
    return pl.BlockSpec(shape, lambda b, t: (0,) * nd)


def _table_spec():
    return pl.BlockSpec((ROW_TILE, LANES), lambda b, t: (t, 0))


def _prev_halo_spec(width):
    return pl.BlockSpec((1, SUBLANES, width),
                        lambda b, t: (b, jnp.maximum(t * HALO_BLOCKS_PER_TILE - 1, 0), 0))


def _next_halo_spec(width, total_rows=T_ALL):
    last = total_rows // SUBLANES - 1
    return pl.BlockSpec((1, SUBLANES, width),
                        lambda b, t: (b, jnp.minimum((t + 1) * HALO_BLOCKS_PER_TILE, last), 0))


def _halo_valid(t):
    prev_ok = jnp.logical_and(t != 0, t != N_LAT_TILES)
    next_ok = jnp.logical_and(t != N_LAT_TILES - 1, t != N_LAT_TILES)
    return prev_ok, next_ok


def _with_halo(prev_ref, main_ref, next_ref, t):
    prev_ok, next_ok = _halo_valid(t)
    return jnp.concatenate([prev_ref[0], main_ref[0], next_ref[0]], axis=0), prev_ok, next_ok


def _halo_row_mask(prev_ok, next_ok):
    r = lax.broadcasted_iota(jnp.int32, (ROW_TILE + 2 * SUBLANES, 1), 0)
    return jnp.logical_and(jnp.logical_or(r >= SUBLANES, prev_ok),
                           jnp.logical_or(r < ROW_TILE + SUBLANES, next_ok))


def _shift_rows(x, k):
    n = x.shape[0]
    return pltpu.roll(x, k % n, axis=0)[SUBLANES:SUBLANES + ROW_TILE]


def _rope_mla(x, c, s_up, s_dn):
    return x * c + pltpu.roll(x, MLA_ROPE // 2, axis=1) * s_up + pltpu.roll(x, LANES - MLA_ROPE // 2, axis=1) * s_dn


def _in0_kernel(x_ref, mod_ref, w_in_ref, gq_ref, w_uq_ref, gkv_ref, w_uk_ref, w_uvt_ref,
                c_ref, su_ref, sd_ref, q_ref, k_ref, v_ref, p_ref):
    m = mod_ref[0, 0]
    h = _bf(_rms(x_ref[0]) * (1.0 + m[1:2]) + m[0:1])
    a = _dot(h, w_in_ref[...])
    i_kv = MLA_Q_RANK
    i_kr = i_kv + MLA_KV_RANK
    i_p = i_kr + LANES
    c, su, sd = c_ref[...], su_ref[...], sd_ref[...]
    scale = (MLA_NOPE + MLA_ROPE) ** -0.5 * LOG2_E

    q = _dot(_bf(_rms(a[:, :i_kv]) * gq_ref[...]), w_uq_ref[...])
    ckv = _bf(_rms(a[:, i_kv:i_kr]) * gkv_ref[...])
    kn = _dot(ckv, w_uk_ref[...])
    kr = _rope_mla(a[:, i_kr:i_p], c, su, sd)
    vt = _dot_nt(w_uvt_ref[...], ckv)
    ones = jnp.ones((ONES_ROWS, vt.shape[1]), jnp.bfloat16)
    for hd in range(MLA_HEADS):
        sl = slice(hd * LANES, (hd + 1) * LANES)
        q_ref[0, :, sl] = _bf(_rope_mla(q[:, sl], c, su, sd) * scale)
        k_ref[0, :, sl] = _bf(kn[:, sl] + kr)
        v_ref[0, hd * MLA_VROWS:hd * MLA_VROWS + MLA_V, :] = _bf(vt[hd * MLA_V:(hd + 1) * MLA_V])
        v_ref[0, hd * MLA_VROWS + MLA_V:(hd + 1) * MLA_VROWS, :] = ones
    p_ref[0] = a[:, i_p:]


def _vt_spec(rows):
    return pl.BlockSpec((1, rows, ROW_TILE), lambda b, t: (b, 0, t))


def _in0(xa, mods, w_in, g_q, w_uq, g_kv, w_uk, w_uvt, tabs):
    B = xa.shape[0]
    outs = (
        jax.ShapeDtypeStruct((B, T_ALL, HEADS_W), jnp.bfloat16),
        jax.ShapeDtypeStruct((B, T_ALL, HEADS_W), jnp.bfloat16),
        jax.ShapeDtypeStruct((B, MLA_HEADS * MLA_VROWS, T_ALL), jnp.bfloat16),
        jax.ShapeDtypeStruct((B, T_ALL, POOL_WIDTH), jnp.float32),
    )
    return pl.pallas_call(
        _in0_kernel,
        out_shape=outs,
        grid=(B, N_TILES),
        in_specs=[
            _row_spec(D_MODEL), _mod_spec(),
            _const_spec(w_in.shape), _const_spec(g_q.shape), _const_spec(w_uq.shape),
            _const_spec(g_kv.shape), _const_spec(w_uk.shape), _const_spec(w_uvt.shape),
            _table_spec(), _table_spec(), _table_spec(),
        ],
        out_specs=(_row_spec(HEADS_W), _row_spec(HEADS_W), _vt_spec(MLA_HEADS * MLA_VROWS),
                   _row_spec(POOL_WIDTH)),
        compiler_params=_params(2),
        name="mla_pool_in_proj",
    )(xa, mods, w_in, g_q, w_uq, g_kv, w_uk, w_uvt, *tabs)


def _in1_kernel(x_ref, mod_ref, w_in_ref, w_vt_ref, gq_ref, gk_ref, c_ref, s_ref, q_ref, k_ref, v_ref):
    m = mod_ref[0, 0]
    h = _bf(_rms(x_ref[0]) * (1.0 + m[1:2]) + m[0:1])
    a = _dot(h, w_in_ref[...])
    c, s = c_ref[...], s_ref[...]
    scale = GQA_HEAD_DIM ** -0.5 * LOG2_E

    def head(j, gain):
        xn = _rms(a[:, j * LANES:(j + 1) * LANES]) * gain
        return xn * c + pltpu.roll(xn, LANES // 2, axis=1) * s

    for j in range(GQA_HEADS):
        q_ref[0, :, j * LANES:(j + 1) * LANES] = _bf(head(j, gq_ref[...]) * scale)
    for j in range(GQA_KV_HEADS):
        k_ref[0, :, j * LANES:(j + 1) * LANES] = _bf(head(GQA_HEADS + j, gk_ref[...]))
    vt = _dot_nt(w_vt_ref[...], h)
    ones = jnp.ones((ONES_ROWS, vt.shape[1]), jnp.bfloat16)
    for j in range(GQA_KV_HEADS):
        v_ref[0, j * GQA_VROWS:j * GQA_VROWS + GQA_HEAD_DIM, :] = _bf(vt[j * GQA_HEAD_DIM:(j + 1) * GQA_HEAD_DIM])
        v_ref[0, j * GQA_VROWS + GQA_HEAD_DIM:(j + 1) * GQA_VROWS, :] = ones


def _in1(xa, mods, w_in, w_vt, g_q, g_k, tabs):
    B = xa.shape[0]
    kv_w = GQA_KV_HEADS * LANES
    outs = (
        jax.ShapeDtypeStruct((B, T_ALL, HEADS_W), jnp.bfloat16),
        jax.ShapeDtypeStruct((B, T_ALL, kv_w), jnp.bfloat16),
        jax.ShapeDtypeStruct((B, GQA_KV_HEADS * GQA_VROWS, T_ALL), jnp.bfloat16),
    )
    return pl.pallas_call(
        _in1_kernel,
        out_shape=outs,
        grid=(B, N_TILES),
        in_specs=[
            _row_spec(D_MODEL), _mod_spec(), _const_spec(w_in.shape), _const_spec(w_vt.shape),
            _const_spec(g_q.shape), _const_spec(g_k.shape), _table_spec(), _table_spec(),
        ],
        out_specs=(_row_spec(HEADS_W), _row_spec(kv_w), _vt_spec(GQA_KV_HEADS * GQA_VROWS)),
        compiler_params=_params(2),
        name="gqa_in_proj",
    )(xa, mods, w_in, w_vt, g_q, g_k, *tabs)


def _score(q, k):
    return _dot_nt(k, q)


def _col_max(s):
    return jnp.max(s, axis=0, keepdims=True)


def _accumulate(s, cm, vt, state):
    m, acc = state
    m_new = jnp.maximum(m, cm)
    p = _bf(jnp.exp2(s - m_new))
    return m_new, jnp.exp2(m - m_new) * acc + _dot(vt, p)


def _attn_kernel(q_ref, k_ref, vt_ref, o_ref, s0_ref, s1_ref, *, n_heads, shared_kv, dv, with_ctx):
    qi = pl.program_id(2)
    tq = q_ref.shape[1]
    vrows = dv + ONES_ROWS
    heads = range(n_heads)
    n_chunks = T_ALL // KV_CHUNK
    assert n_chunks % 2 == 1

    def q_of(hd):
        return q_ref[0, :, hd * LANES:(hd + 1) * LANES]

    def k_of(hd, rows):
        j = 0 if shared_kv else hd
        return k_ref[0, rows, j * LANES:(j + 1) * LANES]

    def vt_of(hd, rows):
        j = 0 if shared_kv else hd
        return vt_ref[0, j * vrows:(j + 1) * vrows, rows]

    def chunk(c):
        return pl.ds(pl.multiple_of(c * KV_CHUNK, KV_CHUNK), KV_CHUNK)

    def init():
        return tuple((jnp.full((1, tq), -jnp.inf, jnp.float32), jnp.zeros((vrows, tq), jnp.float32))
                     for _ in heads)

    def finish(states):
        ot = jnp.concatenate([acc[:dv] / acc[dv:dv + 1] for _, acc in states], axis=0)
        o_ref[0] = _bf(ot.T)

    def latent_rows():
        qs = [q_of(hd) for hd in heads]

        def park(c, hd, buf):
            s = _score(qs[hd], k_of(hd, chunk(c)))
            buf[hd] = s
            return _col_max(s)

        def consume(c, hd, buf, cm, state):
            return _accumulate(buf[hd], cm, vt_of(hd, chunk(c)), state)

        def half_step(c, cur, nxt, cms, states):
            new_cms, new_states = [], []
            for hd in heads:
                new_cms.append(park(c + 1, hd, nxt))
                new_states.append(consume(c, hd, cur, cms[hd], states[hd]))
            return tuple(new_cms), tuple(new_states)

        def body(i, carry):
            cms, states = carry
            cms, states = half_step(2 * i, s0_ref, s1_ref, cms, states)
            return half_step(2 * i + 1, s1_ref, s0_ref, cms, states)

        cms = tuple(park(0, hd, s0_ref) for hd in heads)
        cms, states = lax.fori_loop(0, n_chunks // 2, body, (cms, init()))
        finish([consume(n_chunks - 1, hd, s0_ref, cms[hd], states[hd]) for hd in heads])

    def context_rows():
        rows = pl.ds(SEQ, CTX_LEN)
        states = init()
        out = []
        for hd in heads:
            s = _score(q_of(hd), k_of(hd, rows))
            out.append(_accumulate(s, _col_max(s), vt_of(hd, rows), states[hd]))
        finish(out)

    if with_ctx:
        pl.when(qi < N_LAT_TILES)(latent_rows)
        pl.when(qi == N_LAT_TILES)(context_rows)
    else:
        latent_rows()


def _attention(q, k, vt, *, n_heads, shared_kv, dv, with_ctx):
    B = q.shape[0]
    groups = q.shape[2] // (n_heads * LANES)
    kv_heads = 1 if shared_kv else n_heads
    vrows = dv + ONES_ROWS
    n_q = N_TILES if with_ctx else N_LAT_TILES
    q_spec = pl.BlockSpec((1, ROW_TILE, n_heads * LANES), lambda b, g, i: (b, i, g))
    k_spec = pl.BlockSpec((1, T_ALL, kv_heads * LANES), lambda b, g, i: (b, 0, g))
    vt_spec = pl.BlockSpec((1, kv_heads * vrows, T_ALL), lambda b, g, i: (b, g, 0))
    o_spec = pl.BlockSpec((1, ROW_TILE, n_heads * dv), lambda b, g, i: (b, i, g))
    return pl.pallas_call(
        functools.partial(_attn_kernel, n_heads=n_heads, shared_kv=shared_kv, dv=dv, with_ctx=with_ctx),
        out_shape=jax.ShapeDtypeStruct((B, n_q * ROW_TILE, groups * n_heads * dv), jnp.bfloat16),
        grid=(B, groups, n_q),
        in_specs=[q_spec, k_spec, vt_spec],
        out_specs=o_spec,
        scratch_shapes=[pltpu.VMEM((n_heads, KV_CHUNK, ROW_TILE), jnp.float32)] * 2,
        compiler_params=_params(3),
        name="attention",
    )(q, k, vt)


def _out0_kernel(x_ref, mod_ref, o_ref, pp_ref, p_ref, pn_ref, w_pool_ref, s_pool_ref,
                 w_oa_ref, w_op_ref, y_ref):
    t = pl.program_id(1)
    pe, prev_ok, next_ok = _with_halo(pp_ref, p_ref, pn_ref, t)
    pe = jnp.where(_halo_row_mask(prev_ok, next_ok), pe, 0.0)
    t0 = jnp.where(t == N_LAT_TILES, 0, t * ROW_TILE)
    t_len = jnp.where(t == N_LAT_TILES, CTX_LEN, SEQ)
    pos = t0 + lax.broadcasted_iota(jnp.int32, (ROW_TILE, 1), 0)

    pooled = []
    for g, w in enumerate(POOL_WINDOWS):
        run = pe[:, g * LANES:(g + 1) * LANES]
        span = 1
        while span < w:
            run = run + pltpu.roll(run, run.shape[0] - span, axis=0)
            span *= 2
        win = _shift_rows(run, w // 2)
        lo = jnp.clip(pos - w // 2, 0, t_len)
        hi = jnp.clip(pos - w // 2 + w, 0, t_len)
        d = win / (hi - lo).astype(jnp.float32) - p_ref[0, :, g * LANES:(g + 1) * LANES]
        pooled.append(_dot(_bf(d), w_pool_ref[g]))
    pooled = jnp.concatenate(pooled, axis=-1) * s_pool_ref[...]

    y = _dot(o_ref[0], w_oa_ref[...]) + _dot(_bf(pooled), w_op_ref[...])
    y_ref[0] = x_ref[0] + mod_ref[0, 0][2:3] * y


def _out0(xa, mods, o, p, w_pool, s_pool, w_oa, w_op):
    B = xa.shape[0]
    return pl.pallas_call(
        _out0_kernel,
        out_shape=jax.ShapeDtypeStruct(xa.shape, jnp.float32),
        grid=(B, N_TILES),
        in_specs=[
            _row_spec(D_MODEL), _mod_spec(), _row_spec(o.shape[2]),
            _prev_halo_spec(POOL_WIDTH), _row_spec(POOL_WIDTH), _next_halo_spec(POOL_WIDTH),
            _const_spec(w_pool.shape), _const_spec(s_pool.shape),
            _const_spec(w_oa.shape), _const_spec(w_op.shape),
        ],
        out_specs=_row_spec(D_MODEL),
        compiler_params=_params(2),
        name="mla_pool_out_proj",
    )(xa, mods, o, p, p, p, w_pool, s_pool, w_oa, w_op)


def _out1_kernel(x_ref, mod_ref, o_ref, w_ref, y_ref):
    y_ref[0] = x_ref[0] + mod_ref[0, 0][2:3] * _dot(o_ref[0], w_ref[...])


def _out1(xa, mods, o, w_out):
    B = xa.shape[0]
    return pl.pallas_call(
        _out1_kernel,
        out_shape=jax.ShapeDtypeStruct((B, SEQ, D_MODEL), jnp.float32),
        grid=(B, N_LAT_TILES),
        in_specs=[_row_spec(D_MODEL), _mod_spec(), _row_spec(HEADS_W), _const_spec(w_out.shape)],
        out_specs=_row_spec(D_MODEL),
        compiler_params=_params(2),
        name="gqa_out_proj",
    )(xa, mods, o, w_out)


def _ffn_kernel(xp_ref, x_ref, xn_ref, mod_ref, w_up_ref, cw_ref, cb_ref, w_dn_ref, gf_ref, y_ref,
                *, final_norm):
    t = pl.program_id(1)
    m = mod_ref[0, 0]
    xe, prev_ok, next_ok = _with_halo(xp_ref, x_ref, xn_ref, t)
    he = _rms(xe) * (1.0 + m[4:5]) + m[3:4]
    he = _bf(jnp.where(_halo_row_mask(prev_ok, next_ok), he, 0.0))
    hm = he[SUBLANES:SUBLANES + ROW_TILE]

    y = jnp.zeros((ROW_TILE, D_MODEL), jnp.float32)
    for c0 in range(0, D_FF, FF_CHUNK):
        cs = slice(c0, c0 + FF_CHUNK)
        g = _dot(he, w_up_ref[:, cs])
        u = _dot(hm, w_up_ref[:, D_FF + c0:D_FF + c0 + FF_CHUNK])
        acc = cb_ref[:, cs] + _shift_rows(g, 1) * cw_ref[0:1, cs]
        acc = acc + g[SUBLANES:SUBLANES + ROW_TILE] * cw_ref[1:2, cs]
        acc = acc + _shift_rows(g, -1) * cw_ref[2:3, cs]
        y = y + _dot(_bf(jax.nn.silu(acc) * u), w_dn_ref[cs, :])
    out = x_ref[0] + m[5:6] * y
    if final_norm:
        out = _rms(out) * gf_ref[...]
    y_ref[0] = out


def _ffn(xa, mods, w_up, conv_w, conv_b, w_dn, g_final, *, final_norm):
    B = xa.shape[0]
    rows = xa.shape[1]
    n_t = rows // ROW_TILE
    assert final_norm == (rows == SEQ)
    return pl.pallas_call(
        functools.partial(_ffn_kernel, final_norm=final_norm),
        out_shape=jax.ShapeDtypeStruct((B, rows, D_MODEL), jnp.float32),
        grid=(B, n_t),
        in_specs=[
            _prev_halo_spec(D_MODEL), _row_spec(D_MODEL), _next_halo_spec(D_MODEL, rows), _mod_spec(),
            _const_spec(w_up.shape), _const_spec(conv_w.shape), _const_spec(conv_b.shape),
            _const_spec(w_dn.shape), _const_spec(g_final.shape),
        ],
        out_specs=_row_spec(D_MODEL),
        compiler_params=_params(2),
        name="conv_ffn",
    )(xa, xa, xa, mods, w_up, conv_w, conv_b, w_dn, g_final)


def _rope_angles(rope_dim):
    rows = SEQ // GRID_W
    row = jnp.repeat(jnp.arange(rows, dtype=jnp.float32), GRID_W)
    col = jnp.tile(jnp.arange(GRID_W, dtype=jnp.float32), rows)
    n_freq = rope_dim // 4
    freq = ROPE_THETA ** (-jnp.arange(n_freq, dtype=jnp.float32) / n_freq)
    ang = jnp.concatenate([row[:, None] * freq, col[:, None] * freq], axis=-1)
    return jnp.concatenate([ang, jnp.zeros((CTX_LEN, rope_dim // 2), jnp.float32)], axis=0)


def _mla_tables():
    ang = _rope_angles(MLA_ROPE)
    cos, sin = jnp.cos(ang), jnp.sin(ang)
    half = MLA_ROPE // 2
    ones = jnp.ones((T_ALL, MLA_NOPE), jnp.float32)
    zn = jnp.zeros((T_ALL, MLA_NOPE), jnp.float32)
    zh = jnp.zeros((T_ALL, half), jnp.float32)
    zt = jnp.zeros((T_ALL, LANES - MLA_NOPE - MLA_ROPE), jnp.float32)
    c = jnp.concatenate([ones, cos, cos, zt], axis=-1)
    s_up = jnp.concatenate([zn, zh, sin, zt], axis=-1)
    s_dn = jnp.concatenate([zn, -sin, zh, zt], axis=-1)
    return c, s_up, s_dn


def _gqa_tables():
    ang = _rope_angles(GQA_HEAD_DIM)
    cos, sin = jnp.cos(ang), jnp.sin(ang)
    return jnp.concatenate([cos, cos], axis=-1), jnp.concatenate([-sin, sin], axis=-1)


def _mla_head_cols(w, n_heads, nope, rope):
    K = w.shape[0]
    w = w.reshape(K, n_heads, nope + rope)
    parts = [w[..., :nope]]
    if rope:
        parts += [w[..., nope::2], w[..., nope + 1::2]]
    parts.append(jnp.zeros((K, n_heads, LANES - nope - rope), w.dtype))
    return jnp.concatenate(parts, axis=-1).reshape(K, n_heads * LANES)


def _deinterleave(n):
    return np.concatenate([np.arange(0, n, 2), np.arange(1, n, 2)])


def kernel(x, c, ctx, c_ctx, w_mod, b_mod, mix0_w_in, mla_g_q, mla_w_uq, mla_g_kv, mla_w_uk, mla_w_uv,
           pool_w, pool_scale, mix0_w_out, gqa_w_in, gqa_g_q, gqa_g_k, gqa_w_out,
           ffn_w_up, ffn_conv_w, ffn_conv_b, ffn_w_down, g_final):
    B = x.shape[0]
    assert x.shape == (B, SEQ, D_MODEL) and ctx.shape == (B, CTX_LEN, D_MODEL) and B < SUBLANES
    assert w_mod.shape[0] == 2

    cc = jnp.zeros((SUBLANES, D_MODEL), jnp.float32).at[:B].set(c).at[B].set(c_ctx)
    mod_all = _modulation(cc, w_mod, b_mod).reshape(2, SUBLANES, N_MOD, D_MODEL)

    def mods_of(i):
        lat = mod_all[i, :B]
        con = jnp.broadcast_to(mod_all[i, B], lat.shape)
        return jnp.stack([lat, con], axis=1)

    xa = jnp.concatenate([x, ctx], axis=1)

    w_in = mix0_w_in[0]
    i_kv = MLA_Q_RANK
    i_kr = i_kv + MLA_KV_RANK
    i_p = i_kr + MLA_ROPE
    w_kr = _mla_head_cols(w_in[:, i_kr:i_p], 1, 0, MLA_ROPE)
    w_kr = jnp.roll(w_kr, MLA_NOPE, axis=1)
    w_in0 = _bf(jnp.concatenate([w_in[:, :i_kr], w_kr, w_in[:, i_p:]], axis=1))
    w_uq = _bf(_mla_head_cols(mla_w_uq[0], MLA_HEADS, MLA_NOPE, MLA_ROPE))
    w_uk = _bf(_mla_head_cols(mla_w_uk[0], MLA_HEADS, MLA_NOPE, 0))
    w_uvt = _bf(mla_w_uv[0].T)
    w_out0 = mix0_w_out[0]
    w_oa = _bf(w_out0[:MLA_HEADS * MLA_V])
    w_op = _bf(w_out0[MLA_HEADS * MLA_V:])

    m0 = mods_of(0)
    q, k, vt, p = _in0(xa, m0, w_in0, mla_g_q[0][None], w_uq, mla_g_kv[0][None], w_uk, w_uvt, _mla_tables())
    o = _attention(q, k, vt, n_heads=MLA_HEADS_PER_STEP, shared_kv=False, dv=MLA_V, with_ctx=True)
    xa = _out0(xa, m0, o, p, _bf(pool_w[0]), pool_scale[0][None], w_oa, w_op)
    xa = _ffn(xa, m0, _bf(ffn_w_up[0]), ffn_conv_w[0], ffn_conv_b[0][None], _bf(ffn_w_down[0]),
              g_final[None], final_norm=False)

    perm = _deinterleave(GQA_HEAD_DIM)
    n_qk = GQA_HEADS + GQA_KV_HEADS
    w_in = gqa_w_in[0]
    w_qk = w_in[:, :n_qk * LANES].reshape(D_MODEL, n_qk, LANES)[:, :, perm].reshape(D_MODEL, n_qk * LANES)
    w_vt = _bf(w_in[:, n_qk * LANES:].T)
    m1 = mods_of(1)
    q, k, vt = _in1(xa, m1, _bf(w_qk), w_vt, gqa_g_q[0][perm][None], gqa_g_k[0][perm][None], _gqa_tables())
    o = _attention(q, k, vt, n_heads=GQA_GROUP, shared_kv=True, dv=GQA_HEAD_DIM, with_ctx=False)
    xa = _out1(xa, m1, o, _bf(gqa_w_out[0]))
    return _ffn(xa, m1, _bf(ffn_w_up[1]), ffn_conv_w[1], ffn_conv_b[1][None], _bf(ffn_w_down[1]),
                g_final[None], final_norm=True)
```

```python
import functools

import jax
import jax.numpy as jnp
import numpy as np
from jax import lax
from jax.experimental import pallas as pl
from jax.experimental.pallas import tpu as pltpu

D_MODEL = 1024
SEQ = 8192
GRID_W = 64
CTX_LEN = 256
T_ALL = SEQ + CTX_LEN
ROPE_THETA = 10000.0
EPS = 1e-6
N_MOD = 6
MLA_HEADS = 8
MLA_NOPE = 64
MLA_ROPE = 32
MLA_V = 64
MLA_Q_RANK = 384
MLA_KV_RANK = 256
POOL_WINDOWS = (2, 4, 8, 16)
POOL_WIDTH = 512
GQA_HEADS = 8
GQA_KV_HEADS = 2
GQA_HEAD_DIM = 128
GQA_GROUP = GQA_HEADS // GQA_KV_HEADS
D_FF = 2816
CONV_W = 3

LANES = 128
SUBLANES = 8
ROW_TILE = 256
N_LAT_TILES = SEQ // ROW_TILE
N_TILES = T_ALL // ROW_TILE
HALO_BLOCKS_PER_TILE = ROW_TILE // SUBLANES
KV_CHUNK = 768
FF_CHUNK = 256
VMEM_LIMIT = 56 * 1024 * 1024

MIX0_IN_PAD = MLA_Q_RANK + MLA_KV_RANK + LANES + POOL_WIDTH
HEADS_W = MLA_HEADS * LANES
LOG2_E = 1.4426950408889634
ONES_ROWS = 16
MLA_VROWS = MLA_V + ONES_ROWS
GQA_VROWS = GQA_HEAD_DIM + ONES_ROWS
MLA_HEADS_PER_STEP = 4


def _params(n_axes):
    return pltpu.CompilerParams(dimension_semantics=("arbitrary",) * n_axes, vmem_limit_bytes=VMEM_LIMIT)


def _rms(x):
    return x * lax.rsqrt(jnp.mean(x * x, axis=-1, keepdims=True) + EPS)


def _dot(a, b):
    return jnp.dot(a, b, preferred_element_type=jnp.float32)


def _dot_nt(a, b):
    return lax.dot_general(a, b, (((1,), (1,)), ((), ())), preferred_element_type=jnp.float32)


def _bf(x):
    return x.astype(jnp.bfloat16)


def _mod_kernel(c_ref, w_ref, b_ref, o_ref):
    s = jax.nn.silu(c_ref[...])
    o_ref[0] = jnp.dot(s, w_ref[0], preferred_element_type=jnp.float32,
                       precision=lax.Precision.HIGHEST) + b_ref[0]


def _modulation(cc, w_mod, b_mod):
    depth = w_mod.shape[0]
    return pl.pallas_call(
        _mod_kernel,
        out_shape=jax.ShapeDtypeStruct((depth, SUBLANES, N_MOD * D_MODEL), jnp.float32),
        grid=(depth, N_MOD),
        in_specs=[
            pl.BlockSpec((SUBLANES, D_MODEL), lambda i, j: (0, 0)),
            pl.BlockSpec((1, D_MODEL, D_MODEL), lambda i, j: (i, 0, j)),
            pl.BlockSpec((1, 1, D_MODEL), lambda i, j: (i, 0, j)),
        ],
        out_specs=pl.BlockSpec((1, SUBLANES, D_MODEL), lambda i, j: (i, 0, j)),
        compiler_params=_params(2),
        name="modulation",
    )(cc, w_mod, b_mod.reshape(depth, 1, N_MOD * D_MODEL))


def _row_spec(width, rows=ROW_TILE):
    return pl.BlockSpec((1, rows, width), lambda b, t: (b, t, 0))


def _mod_spec():
    return pl.BlockSpec((1, 1, N_MOD, D_MODEL), lambda b, t: (b, t // N_LAT_TILES, 0, 0))


def _const_spec(shape):
    nd = len(shape)
    return pl.BlockSpec(shape, lambda b, t: (0,) * nd)


def _table_spec():
    return pl.BlockSpec((ROW_TILE, LANES), lambda b, t: (t, 0))


def _prev_halo_spec(width):
    return pl.BlockSpec((1, SUBLANES, width),
                        lambda b, t: (b, jnp.maximum(t * HALO_BLOCKS_PER_TILE - 1, 0), 0))


def _next_halo_spec(width, total_rows=T_ALL):
    last = total_rows // SUBLANES - 1
    return pl.BlockSpec((1, SUBLANES, width),
                        lambda b, t: (b, jnp.minimum((t + 1) * HALO_BLOCKS_PER_TILE, last), 0))


def _halo_valid(t):
    prev_ok = jnp.logical_and(t != 0, t != N_LAT_TILES)
    next_ok = jnp.logical_and(t != N_LAT_TILES - 1, t != N_LAT_TILES)
    return prev_ok, next_ok


def _with_halo(prev_ref, main_ref, next_ref, t):
    prev_ok, next_ok = _halo_valid(t)
    return jnp.concatenate([prev_ref[0], main_ref[0], next_ref[0]], axis=0), prev_ok, next_ok


def _halo_row_mask(prev_ok, next_ok):
    r = lax.broadcasted_iota(jnp.int32, (ROW_TILE + 2 * SUBLANES, 1), 0)
    return jnp.logical_and(jnp.logical_or(r >= SUBLANES, prev_ok),
                           jnp.logical_or(r < ROW_TILE + SUBLANES, next_ok))


def _shift_rows(x, k):
    n = x.shape[0]
    return pltpu.roll(x, k % n, axis=0)[SUBLANES:SUBLANES + ROW_TILE]


def _rope_mla(x, c, s_up, s_dn):
    return x * c + pltpu.roll(x, MLA_ROPE // 2, axis=1) * s_up + pltpu.roll(x, LANES - MLA_ROPE // 2, axis=1) * s_dn


def _two_stream_specs():
    return [pl.BlockSpec((1, ROW_TILE, D_MODEL), lambda b, t: (b, jnp.minimum(t, N_LAT_TILES - 1), 0)),
            pl.BlockSpec((1, CTX_LEN, D_MODEL), lambda b, t: (b, 0, 0))]


def _two_stream_rows(x_ref, ctx_ref):
    return jnp.where(pl.program_id(1) == N_LAT_TILES, ctx_ref[0], x_ref[0])


def _in0_kernel(x_ref, ctx_ref, mod_ref, w_in_ref, gq_ref, w_uq_ref, gkv_ref, w_uk_ref, w_uvt_ref,
                c_ref, su_ref, sd_ref, q_ref, k_ref, v_ref, p_ref):
    m = mod_ref[0, 0]
    h = _bf(_rms(_two_stream_rows(x_ref, ctx_ref)) * (1.0 + m[1:2]) + m[0:1])
    a = _dot(h, w_in_ref[...])
    i_kv = MLA_Q_RANK
    i_kr = i_kv + MLA_KV_RANK
    i_p = i_kr + LANES
    c, su, sd = c_ref[...], su_ref[...], sd_ref[...]
    scale = (MLA_NOPE + MLA_ROPE) ** -0.5 * LOG2_E

    q = _dot(_bf(_rms(a[:, :i_kv]) * gq_ref[...]), w_uq_ref[...])
    ckv = _bf(_rms(a[:, i_kv:i_kr]) * gkv_ref[...])
    kn = _dot(ckv, w_uk_ref[...])
    kr = _rope_mla(a[:, i_kr:i_p], c, su, sd)
    vt = _dot_nt(w_uvt_ref[...], ckv)
    ones = jnp.ones((ONES_ROWS, vt.shape[1]), jnp.bfloat16)
    for hd in range(MLA_HEADS):
        sl = slice(hd * LANES, (hd + 1) * LANES)
        q_ref[0, :, sl] = _bf(_rope_mla(q[:, sl], c, su, sd) * scale)
        k_ref[0, :, sl] = _bf(kn[:, sl] + kr)
        v_ref[0, hd * MLA_VROWS:hd * MLA_VROWS + MLA_V, :] = _bf(vt[hd * MLA_V:(hd + 1) * MLA_V])
        v_ref[0, hd * MLA_VROWS + MLA_V:(hd + 1) * MLA_VROWS, :] = ones
    p_ref[0] = a[:, i_p:]


def _vt_spec(rows):
    return pl.BlockSpec((1, rows, ROW_TILE), lambda b, t: (b, 0, t))


def _in0(x, ctx, mods, w_in, g_q, w_uq, g_kv, w_uk, w_uvt, tabs):
    B = x.shape[0]
    outs = (
        jax.ShapeDtypeStruct((B, T_ALL, HEADS_W), jnp.bfloat16),
        jax.ShapeDtypeStruct((B, T_ALL, HEADS_W), jnp.bfloat16),
        jax.ShapeDtypeStruct((B, MLA_HEADS * MLA_VROWS, T_ALL), jnp.bfloat16),
        jax.ShapeDtypeStruct((B, T_ALL, POOL_WIDTH), jnp.float32),
    )
    return pl.pallas_call(
        _in0_kernel,
        out_shape=outs,
        grid=(B, N_TILES),
        in_specs=_two_stream_specs() + [
            _mod_spec(),
            _const_spec(w_in.shape), _const_spec(g_q.shape), _const_spec(w_uq.shape),
            _const_spec(g_kv.shape), _const_spec(w_uk.shape), _const_spec(w_uvt.shape),
            _table_spec(), _table_spec(), _table_spec(),
        ],
        out_specs=(_row_spec(HEADS_W), _row_spec(HEADS_W), _vt_spec(MLA_HEADS * MLA_VROWS),
                   _row_spec(POOL_WIDTH)),
        compiler_params=_params(2),
        name="mla_pool_in_proj",
    )(x, ctx, mods, w_in, g_q, w_uq, g_kv, w_uk, w_uvt, *tabs)


def _in1_kernel(x_ref, mod_ref, w_in_ref, w_vt_ref, gq_ref, gk_ref, c_ref, s_ref, q_ref, k_ref, v_ref):
    m = mod_ref[0, 0]
    h = _bf(_rms(x_ref[0]) * (1.0 + m[1:2]) + m[0:1])
    a = _dot(h, w_in_ref[...])
    c, s = c_ref[...], s_ref[...]
    scale = GQA_HEAD_DIM ** -0.5 * LOG2_E

    def head(j, gain):
        xn = _rms(a[:, j * LANES:(j + 1) * LANES]) * gain
        return xn * c + pltpu.roll(xn, LANES // 2, axis=1) * s

    for j in range(GQA_HEADS):
        q_ref[0, :, j * LANES:(j + 1) * LANES] = _bf(head(j, gq_ref[...]) * scale)
    for j in range(GQA_KV_HEADS):
        k_ref[0, :, j * LANES:(j + 1) * LANES] = _bf(head(GQA_HEADS + j, gk_ref[...]))
    vt = _dot_nt(w_vt_ref[...], h)
    ones = jnp.ones((ONES_ROWS, vt.shape[1]), jnp.bfloat16)
    for j in range(GQA_KV_HEADS):
        v_ref[0, j * GQA_VROWS:j * GQA_VROWS + GQA_HEAD_DIM, :] = _bf(vt[j * GQA_HEAD_DIM:(j + 1) * GQA_HEAD_DIM])
        v_ref[0, j * GQA_VROWS + GQA_HEAD_DIM:(j + 1) * GQA_VROWS, :] = ones


def _in1(xa, mods, w_in, w_vt, g_q, g_k, tabs):
    B = xa.shape[0]
    kv_w = GQA_KV_HEADS * LANES
    outs = (
        jax.ShapeDtypeStruct((B, T_ALL, HEADS_W), jnp.bfloat16),
        jax.ShapeDtypeStruct((B, T_ALL, kv_w), jnp.bfloat16),
        jax.ShapeDtypeStruct((B, GQA_KV_HEADS * GQA_VROWS, T_ALL), jnp.bfloat16),
    )
    return pl.pallas_call(
        _in1_kernel,
        out_shape=outs,
        grid=(B, N_TILES),
        in_specs=[
            _row_spec(D_MODEL), _mod_spec(), _const_spec(w_in.shape), _const_spec(w_vt.shape),
            _const_spec(g_q.shape), _const_spec(g_k.shape), _table_spec(), _table_spec(),
        ],
        out_specs=(_row_spec(HEADS_W), _row_spec(kv_w), _vt_spec(GQA_KV_HEADS * GQA_VROWS)),
        compiler_params=_params(2),
        name="gqa_in_proj",
    )(xa, mods, w_in, w_vt, g_q, g_k, *tabs)


def _score(q, k):
    return _dot_nt(k, q)


def _col_max(s):
    return jnp.max(s, axis=0, keepdims=True)


def _accumulate(s, cm, vt, state):
    m, acc = state
    m_new = jnp.maximum(m, cm)
    p = _bf(jnp.exp2(s - m_new))
    return m_new, jnp.exp2(m - m_new) * acc + _dot(vt, p)


def _attn_kernel(q_ref, k_ref, vt_ref, o_ref, s0_ref, s1_ref, *, n_heads, shared_kv, dv, with_ctx):
    qi = pl.program_id(2)
    tq = q_ref.shape[1]
    vrows = dv + ONES_ROWS
    heads = range(n_heads)
    n_chunks = T_ALL // KV_CHUNK
    assert n_chunks % 2 == 1

    def q_of(hd):
        return q_ref[0, :, hd * LANES:(hd + 1) * LANES]

    def k_of(hd, rows):
        j = 0 if shared_kv else hd
        return k_ref[0, rows, j * LANES:(j + 1) * LANES]

    def vt_of(hd, rows):
        j = 0 if shared_kv else hd
        return vt_ref[0, j * vrows:(j + 1) * vrows, rows]

    def chunk(c):
        return pl.ds(pl.multiple_of(c * KV_CHUNK, KV_CHUNK), KV_CHUNK)

    def init():
        return tuple((jnp.full((1, tq), -jnp.inf, jnp.float32), jnp.zeros((vrows, tq), jnp.float32))
                     for _ in heads)

    def finish(states):
        ot = jnp.concatenate([acc[:dv] / acc[dv:dv + 1] for _, acc in states], axis=0)
        o_ref[0] = _bf(ot.T)

    def latent_rows():
        qs = [q_of(hd) for hd in heads]

        def park(c, hd, buf):
            s = _score(qs[hd], k_of(hd, chunk(c)))
            buf[hd] = s
            return _col_max(s)

        def consume(c, hd, buf, cm, state):
            return _accumulate(buf[hd], cm, vt_of(hd, chunk(c)), state)

        def half_step(c, cur, nxt, cms, states):
            new_cms, new_states = [], []
            for hd in heads:
                new_cms.append(park(c + 1, hd, nxt))
                new_states.append(consume(c, hd, cur, cms[hd], states[hd]))
            return tuple(new_cms), tuple(new_states)

        def body(i, carry):
            cms, states = carry
            cms, states = half_step(2 * i, s0_ref, s1_ref, cms, states)
            return half_step(2 * i + 1, s1_ref, s0_ref, cms, states)

        cms = tuple(park(0, hd, s0_ref) for hd in heads)
        cms, states = lax.fori_loop(0, n_chunks // 2, body, (cms, init()))
        finish([consume(n_chunks - 1, hd, s0_ref, cms[hd], states[hd]) for hd in heads])

    def context_rows():
        rows = pl.ds(SEQ, CTX_LEN)
        states = init()
        out = []
        for hd in heads:
            s = _score(q_of(hd), k_of(hd, rows))
            out.append(_accumulate(s, _col_max(s), vt_of(hd, rows), states[hd]))
        finish(out)

    if with_ctx:
        pl.when(qi < N_LAT_TILES)(latent_rows)
        pl.when(qi == N_LAT_TILES)(context_rows)
    else:
        latent_rows()


def _attention(q, k, vt, *, n_heads, shared_kv, dv, with_ctx):
    B = q.shape[0]
    groups = q.shape[2] // (n_heads * LANES)
    kv_heads = 1 if shared_kv else n_heads
    vrows = dv + ONES_ROWS
    n_q = N_TILES if with_ctx else N_LAT_TILES
    q_spec = pl.BlockSpec((1, ROW_TILE, n_heads * LANES), lambda b, g, i: (b, i, g))
    k_spec = pl.BlockSpec((1, T_ALL, kv_heads * LANES), lambda b, g, i: (b, 0, g))
    vt_spec = pl.BlockSpec((1, kv_heads * vrows, T_ALL), lambda b, g, i: (b, g, 0))
    o_spec = pl.BlockSpec((1, ROW_TILE, n_heads * dv), lambda b, g, i: (b, i, g))
    return pl.pallas_call(
        functools.partial(_attn_kernel, n_heads=n_heads, shared_kv=shared_kv, dv=dv, with_ctx=with_ctx),
        out_shape=jax.ShapeDtypeStruct((B, n_q * ROW_TILE, groups * n_heads * dv), jnp.bfloat16),
        grid=(B, groups, n_q),
        in_specs=[q_spec, k_spec, vt_spec],
        out_specs=o_spec,
        scratch_shapes=[pltpu.VMEM((n_heads, KV_CHUNK, ROW_TILE), jnp.float32)] * 2,
        compiler_params=_params(3),
        name="attention",
    )(q, k, vt)


def _out0_kernel(x_ref, ctx_ref, mod_ref, o_ref, pp_ref, p_ref, pn_ref, w_pool_ref, s_pool_ref,
                 w_oa_ref, w_op_ref, y_ref):
    t = pl.program_id(1)
    pe, prev_ok, next_ok = _with_halo(pp_ref, p_ref, pn_ref, t)
    pe = jnp.where(_halo_row_mask(prev_ok, next_ok), pe, 0.0)
    t0 = jnp.where(t == N_LAT_TILES, 0, t * ROW_TILE)
    t_len = jnp.where(t == N_LAT_TILES, CTX_LEN, SEQ)
    pos = t0 + lax.broadcasted_iota(jnp.int32, (ROW_TILE, 1), 0)

    pooled = []
    for g, w in enumerate(POOL_WINDOWS):
        run = pe[:, g * LANES:(g + 1) * LANES]
        span = 1
        while span < w:
            run = run + pltpu.roll(run, run.shape[0] - span, axis=0)
            span *= 2
        win = _shift_rows(run, w // 2)
        lo = jnp.clip(pos - w // 2, 0, t_len)
        hi = jnp.clip(pos - w // 2 + w, 0, t_len)
        d = win / (hi - lo).astype(jnp.float32) - p_ref[0, :, g * LANES:(g + 1) * LANES]
        pooled.append(_dot(_bf(d), w_pool_ref[g]))
    pooled = jnp.concatenate(pooled, axis=-1) * s_pool_ref[...]

    y = _dot(o_ref[0], w_oa_ref[...]) + _dot(_bf(pooled), w_op_ref[...])
    y_ref[0] = _two_stream_rows(x_ref, ctx_ref) + mod_ref[0, 0][2:3] * y


def _out0(x, ctx, mods, o, p, w_pool, s_pool, w_oa, w_op):
    B = x.shape[0]
    return pl.pallas_call(
        _out0_kernel,
        out_shape=jax.ShapeDtypeStruct((B, T_ALL, D_MODEL), jnp.float32),
        grid=(B, N_TILES),
        in_specs=_two_stream_specs() + [
            _mod_spec(), _row_spec(o.shape[2]),
            _prev_halo_spec(POOL_WIDTH), _row_spec(POOL_WIDTH), _next_halo_spec(POOL_WIDTH),
            _const_spec(w_pool.shape), _const_spec(s_pool.shape),
            _const_spec(w_oa.shape), _const_spec(w_op.shape),
        ],
        out_specs=_row_spec(D_MODEL),
        compiler_params=_params(2),
        name="mla_pool_out_proj",
    )(x, ctx, mods, o, p, p, p, w_pool, s_pool, w_oa, w_op)


def _out1_kernel(x_ref, mod_ref, o_ref, w_ref, y_ref):
    y_ref[0] = x_ref[0] + mod_ref[0, 0][2:3] * _dot(o_ref[0], w_ref[...])


def _out1(xa, mods, o, w_out):
    B = xa.shape[0]
    return pl.pallas_call(
        _out1_kernel,
        out_shape=jax.ShapeDtypeStruct((B, SEQ, D_MODEL), jnp.float32),
        grid=(B, N_LAT_TILES),
        in_specs=[_row_spec(D_MODEL), _mod_spec(), _row_spec(HEADS_W), _const_spec(w_out.shape)],
        out_specs=_row_spec(D_MODEL),
        compiler_params=_params(2),
        name="gqa_out_proj",
    )(xa, mods, o, w_out)


def _ffn_kernel(xp_ref, x_ref, xn_ref, mod_ref, w_up_ref, cw_ref, cb_ref, w_dn_ref, gf_ref, y_ref,
                *, final_norm):
    t = pl.program_id(1)
    m = mod_ref[0, 0]
    xe, prev_ok, next_ok = _with_halo(xp_ref, x_ref, xn_ref, t)
    he = _rms(xe) * (1.0 + m[4:5]) + m[3:4]
    he = _bf(jnp.where(_halo_row_mask(prev_ok, next_ok), he, 0.0))
    hm = he[SUBLANES:SUBLANES + ROW_TILE]

    def up(cs):
        g = _dot(he, w_up_ref[:, cs])
        u = _dot(hm, w_up_ref[:, D_FF + cs.start:D_FF + cs.stop])
        return g, u

    def down(cs, g, u):
        acc = cb_ref[:, cs] + _shift_rows(g, 1) * cw_ref[0:1, cs]
        acc = acc + g[SUBLANES:SUBLANES + ROW_TILE] * cw_ref[1:2, cs]
        acc = acc + _shift_rows(g, -1) * cw_ref[2:3, cs]
        return _dot(_bf(jax.nn.silu(acc) * u), w_dn_ref[cs, :])

    chunks = [slice(c0, c0 + FF_CHUNK) for c0 in range(0, D_FF, FF_CHUNK)]
    y = jnp.zeros((ROW_TILE, D_MODEL), jnp.float32)
    pending = up(chunks[0])
    for cs, cs_next in zip(chunks, chunks[1:] + [None]):
        ahead = up(cs_next) if cs_next is not None else None
        y = y + down(cs, *pending)
        pending = ahead
    out = x_ref[0] + m[5:6] * y
    if final_norm:
        out = _rms(out) * gf_ref[...]
    y_ref[0] = out


def _ffn(xa, mods, w_up, conv_w, conv_b, w_dn, g_final, *, final_norm):
    B = xa.shape[0]
    rows = xa.shape[1]
    n_t = rows // ROW_TILE
    assert final_norm == (rows == SEQ)
    return pl.pallas_call(
        functools.partial(_ffn_kernel, final_norm=final_norm),
        out_shape=jax.ShapeDtypeStruct((B, rows, D_MODEL), jnp.float32),
        grid=(B, n_t),
        in_specs=[
            _prev_halo_spec(D_MODEL), _row_spec(D_MODEL), _next_halo_spec(D_MODEL, rows), _mod_spec(),
            _const_spec(w_up.shape), _const_spec(conv_w.shape), _const_spec(conv_b.shape),
            _const_spec(w_dn.shape), _const_spec(g_final.shape),
        ],
        out_specs=_row_spec(D_MODEL),
        compiler_params=_params(2),
        name="conv_ffn",
    )(xa, xa, xa, mods, w_up, conv_w, conv_b, w_dn, g_final)


def _rope_angles(rope_dim):
    rows = SEQ // GRID_W
    row = jnp.repeat(jnp.arange(rows, dtype=jnp.float32), GRID_W)
    col = jnp.tile(jnp.arange(GRID_W, dtype=jnp.float32), rows)
    n_freq = rope_dim // 4
    freq = ROPE_THETA ** (-jnp.arange(n_freq, dtype=jnp.float32) / n_freq)
    ang = jnp.concatenate([row[:, None] * freq, col[:, None] * freq], axis=-1)
    return jnp.concatenate([ang, jnp.zeros((CTX_LEN, rope_dim // 2), jnp.float32)], axis=0)


def _mla_tables():
    ang = _rope_angles(MLA_ROPE)
    cos, sin = jnp.cos(ang), jnp.sin(ang)
    half = MLA_ROPE // 2
    ones = jnp.ones((T_ALL, MLA_NOPE), jnp.float32)
    zn = jnp.zeros((T_ALL, MLA_NOPE), jnp.float32)
    zh = jnp.zeros((T_ALL, half), jnp.float32)
    zt = jnp.zeros((T_ALL, LANES - MLA_NOPE - MLA_ROPE), jnp.float32)
    c = jnp.concatenate([ones, cos, cos, zt], axis=-1)
    s_up = jnp.concatenate([zn, zh, sin, zt], axis=-1)
    s_dn = jnp.concatenate([zn, -sin, zh, zt], axis=-1)
    return c, s_up, s_dn


def _gqa_tables():
    ang = _rope_angles(GQA_HEAD_DIM)
    cos, sin = jnp.cos(ang), jnp.sin(ang)
    return jnp.concatenate([cos, cos], axis=-1), jnp.concatenate([-sin, sin], axis=-1)


def _mla_head_cols(w, n_heads, nope, rope):
    K = w.shape[0]
    w = w.reshape(K, n_heads, nope + rope)
    parts = [w[..., :nope]]
    if rope:
        parts += [w[..., nope::2], w[..., nope + 1::2]]
    parts.append(jnp.zeros((K, n_heads, LANES - nope - rope), w.dtype))
    return jnp.concatenate(parts, axis=-1).reshape(K, n_heads * LANES)


def _deinterleave(n):
    return np.concatenate([np.arange(0, n, 2), np.arange(1, n, 2)])


def kernel(x, c, ctx, c_ctx, w_mod, b_mod, mix0_w_in, mla_g_q, mla_w_uq, mla_g_kv, mla_w_uk, mla_w_uv,
           pool_w, pool_scale, mix0_w_out, gqa_w_in, gqa_g_q, gqa_g_k, gqa_w_out,
           ffn_w_up, ffn_conv_w, ffn_conv_b, ffn_w_down, g_final):
    B = x.shape[0]
    assert x.shape == (B, SEQ, D_MODEL) and ctx.shape == (B, CTX_LEN, D_MODEL) and B < SUBLANES
    assert w_mod.shape[0] == 2

    cc = jnp.zeros((SUBLANES, D_MODEL), jnp.float32).at[:B].set(c).at[B].set(c_ctx)
    mod_all = _modulation(cc, w_mod, b_mod).reshape(2, SUBLANES, N_MOD, D_MODEL)

    def mods_of(i):
        lat = mod_all[i, :B]
        con = jnp.broadcast_to(mod_all[i, B], lat.shape)
        return jnp.stack([lat, con], axis=1)

    w_in = mix0_w_in[0]
    i_kv = MLA_Q_RANK
    i_kr = i_kv + MLA_KV_RANK
    i_p = i_kr + MLA_ROPE
    w_kr = _mla_head_cols(w_in[:, i_kr:i_p], 1, 0, MLA_ROPE)
    w_kr = jnp.roll(w_kr, MLA_NOPE, axis=1)
    w_in0 = _bf(jnp.concatenate([w_in[:, :i_kr], w_kr, w_in[:, i_p:]], axis=1))
    w_uq = _bf(_mla_head_cols(mla_w_uq[0], MLA_HEADS, MLA_NOPE, MLA_ROPE))
    w_uk = _bf(_mla_head_cols(mla_w_uk[0], MLA_HEADS, MLA_NOPE, 0))
    w_uvt = _bf(mla_w_uv[0].T)
    w_out0 = mix0_w_out[0]
    w_oa = _bf(w_out0[:MLA_HEADS * MLA_V])
    w_op = _bf(w_out0[MLA_HEADS * MLA_V:])

    m0 = mods_of(0)
    q, k, vt, p = _in0(x, ctx, m0, w_in0, mla_g_q[0][None], w_uq, mla_g_kv[0][None], w_uk, w_uvt,
                       _mla_tables())
    o = _attention(q, k, vt, n_heads=MLA_HEADS_PER_STEP, shared_kv=False, dv=MLA_V, with_ctx=True)
    xa = _out0(x, ctx, m0, o, p, _bf(pool_w[0]), pool_scale[0][None], w_oa, w_op)
    xa = _ffn(xa, m0, _bf(ffn_w_up[0]), ffn_conv_w[0], ffn_conv_b[0][None], _bf(ffn_w_down[0]),
              g_final[None], final_norm=False)

    perm = _deinterleave(GQA_HEAD_DIM)
    n_qk = GQA_HEADS + GQA_KV_HEADS
    w_in = gqa_w_in[0]
    w_qk = w_in[:, :n_qk * LANES].reshape(D_MODEL, n_qk, LANES)[:, :, perm].reshape(D_MODEL, n_qk * LANES)
    w_vt = _bf(w_in[:, n_qk * LANES:].T)
    m1 = mods_of(1)
    q, k, vt = _in1(xa, m1, _bf(w_qk), w_vt, gqa_g_q[0][perm][None], gqa_g_k[0][perm][None], _gqa_tables())
    o = _attention(q, k, vt, n_heads=GQA_GROUP, shared_kv=True, dv=GQA_HEAD_DIM, with_ctx=False)
    xa = _out1(xa, m1, o, _bf(gqa_w_out[0]))
    return _ffn(xa, m1, _bf(ffn_w_up[1]), ffn_conv_w[1], ffn_conv_b[1][None], _bf(ffn_w_down[1]),
                g_final[None], final_norm=True)
```

```python
import functools

import jax
import jax.numpy as jnp
import numpy as np
from jax import lax
from jax.experimental import pallas as pl
from jax.experimental.pallas import tpu as pltpu

D_MODEL = 1024
SEQ = 8192
GRID_W = 64
CTX_LEN = 256
T_ALL = SEQ + CTX_LEN
ROPE_THETA = 10000.0
EPS = 1e-6
N_MOD = 6
MLA_HEADS = 8
MLA_NOPE = 64
MLA_ROPE = 32
MLA_V = 64
MLA_Q_RANK = 384
MLA_KV_RANK = 256
POOL_WINDOWS = (2, 4, 8, 16)
POOL_WIDTH = 512
GQA_HEADS = 8
GQA_KV_HEADS = 2
GQA_HEAD_DIM = 128
GQA_GROUP = GQA_HEADS // GQA_KV_HEADS
D_FF = 2816
CONV_W = 3

LANES = 128
SUBLANES = 8
ROW_TILE = 256
N_LAT_TILES = SEQ // ROW_TILE
N_TILES = T_ALL // ROW_TILE
HALO_BLOCKS_PER_TILE = ROW_TILE // SUBLANES
KV_CHUNK = 768
PAIRS_PER_TRIP = 2
FF_CHUNK = 256
VMEM_LIMIT = 56 * 1024 * 1024

MIX0_IN_PAD = MLA_Q_RANK + MLA_KV_RANK + LANES + POOL_WIDTH
HEADS_W = MLA_HEADS * LANES
LOG2_E = 1.4426950408889634
ONES_ROWS = 16
MLA_VROWS = MLA_V + ONES_ROWS
GQA_VROWS = GQA_HEAD_DIM + ONES_ROWS
MLA_HEADS_PER_STEP = 4


def _params(n_axes):
    return pltpu.CompilerParams(dimension_semantics=("arbitrary",) * n_axes, vmem_limit_bytes=VMEM_LIMIT)


def _rms(x):
    return x * lax.rsqrt(jnp.mean(x * x, axis=-1, keepdims=True) + EPS)


def _dot(a, b):
    return jnp.dot(a, b, preferred_element_type=jnp.float32)


def _dot_nt(a, b):
    return lax.dot_general(a, b, (((1,), (1,)), ((), ())), preferred_element_type=jnp.float32)


def _bf(x):
    return x.astype(jnp.bfloat16)


def _mod_kernel(c_ref, w_ref, b_ref, o_ref):
    s = jax.nn.silu(c_ref[...])
    o_ref[0] = jnp.dot(s, w_ref[0], preferred_element_type=jnp.float32,
                       precision=lax.Precision.HIGHEST) + b_ref[0]


def _modulation(cc, w_mod, b_mod):
    depth = w_mod.shape[0]
    return pl.pallas_call(
        _mod_kernel,
        out_shape=jax.ShapeDtypeStruct((depth, SUBLANES, N_MOD * D_MODEL), jnp.float32),
        grid=(depth, N_MOD),
        in_specs=[
            pl.BlockSpec((SUBLANES, D_MODEL), lambda i, j: (0, 0)),
            pl.BlockSpec((1, D_MODEL, D_MODEL), lambda i, j: (i, 0, j)),
            pl.BlockSpec((1, 1, D_MODEL), lambda i, j: (i, 0, j)),
        ],
        out_specs=pl.BlockSpec((1, SUBLANES, D_MODEL), lambda i, j: (i, 0, j)),
        compiler_params=_params(2),
        name="modulation",
    )(cc, w_mod, b_mod.reshape(depth, 1, N_MOD * D_MODEL))


def _row_spec(width, rows=ROW_TILE):
    return pl.BlockSpec((1, rows, width), lambda b, t: (b, t, 0))


def _mod_spec():
    return pl.BlockSpec((1, 1, N_MOD, D_MODEL), lambda b, t: (b, t // N_LAT_TILES, 0, 0))


def _const_spec(shape):
    nd = len(shape)
    return pl.BlockSpec(shape, lambda b, t: (0,) * nd)


def _table_spec():
    return pl.BlockSpec((ROW_TILE, LANES), lambda b, t: (t, 0))


def _prev_halo_spec(width):
    return pl.BlockSpec((1, SUBLANES, width),
                        lambda b, t: (b, jnp.maximum(t * HALO_BLOCKS_PER_TILE - 1, 0), 0))


def _next_halo_spec(width, total_rows=T_ALL):
    last = total_rows // SUBLANES - 1
    return pl.BlockSpec((1, SUBLANES, width),
                        lambda b, t: (b, jnp.minimum((t + 1) * HALO_BLOCKS_PER_TILE, last), 0))


def _halo_valid(t):
    prev_ok = jnp.logical_and(t != 0, t != N_LAT_TILES)
    next_ok = jnp.logical_and(t != N_LAT_TILES - 1, t != N_LAT_TILES)
    return prev_ok, next_ok


def _with_halo(prev_ref, main_ref, next_ref, t):
    prev_ok, next_ok = _halo_valid(t)
    return jnp.concatenate([prev_ref[0], main_ref[0], next_ref[0]], axis=0), prev_ok, next_ok


def _halo_row_mask(prev_ok, next_ok):
    r = lax.broadcasted_iota(jnp.int32, (ROW_TILE + 2 * SUBLANES, 1), 0)
    return jnp.logical_and(jnp.logical_or(r >= SUBLANES, prev_ok),
                           jnp.logical_or(r < ROW_TILE + SUBLANES, next_ok))


def _shift_rows(x, k):
    n = x.shape[0]
    return pltpu.roll(x, k % n, axis=0)[SUBLANES:SUBLANES + ROW_TILE]


def _rope_mla(x, c, s_up, s_dn):
    return x * c + pltpu.roll(x, MLA_ROPE // 2, axis=1) * s_up + pltpu.roll(x, LANES - MLA_ROPE // 2, axis=1) * s_dn


def _two_stream_specs():
    return [pl.BlockSpec((1, ROW_TILE, D_MODEL), lambda b, t: (b, jnp.minimum(t, N_LAT_TILES - 1), 0)),
            pl.BlockSpec((1, CTX_LEN, D_MODEL), lambda b, t: (b, 0, 0))]


def _two_stream_rows(x_ref, ctx_ref):
    return jnp.where(pl.program_id(1) == N_LAT_TILES, ctx_ref[0], x_ref[0])


def _in0_kernel(x_ref, ctx_ref, mod_ref, w_in_ref, gq_ref, w_uq_ref, gkv_ref, w_uk_ref, w_uvt_ref,
                c_ref, su_ref, sd_ref, q_ref, k_ref, v_ref, p_ref):
    m = mod_ref[0, 0]
    h = _bf(_rms(_two_stream_rows(x_ref, ctx_ref)) * (1.0 + m[1:2]) + m[0:1])
    a = _dot(h, w_in_ref[...])
    i_kv = MLA_Q_RANK
    i_kr = i_kv + MLA_KV_RANK
    i_p = i_kr + LANES
    c, su, sd = c_ref[...], su_ref[...], sd_ref[...]
    scale = (MLA_NOPE + MLA_ROPE) ** -0.5 * LOG2_E

    q = _dot(_bf(_rms(a[:, :i_kv]) * gq_ref[...]), w_uq_ref[...])
    ckv = _bf(_rms(a[:, i_kv:i_kr]) * gkv_ref[...])
    kn = _dot(ckv, w_uk_ref[...])
    kr = _rope_mla(a[:, i_kr:i_p], c, su, sd)
    vt = _dot_nt(w_uvt_ref[...], ckv)
    ones = jnp.ones((ONES_ROWS, vt.shape[1]), jnp.bfloat16)
    for hd in range(MLA_HEADS):
        sl = slice(hd * LANES, (hd + 1) * LANES)
        q_ref[0, :, sl] = _bf(_rope_mla(q[:, sl], c, su, sd) * scale)
        k_ref[0, :, sl] = _bf(kn[:, sl] + kr)
        v_ref[0, hd * MLA_VROWS:hd * MLA_VROWS + MLA_V, :] = _bf(vt[hd * MLA_V:(hd + 1) * MLA_V])
        v_ref[0, hd * MLA_VROWS + MLA_V:(hd + 1) * MLA_VROWS, :] = ones
    p_ref[0] = a[:, i_p:]


def _vt_spec(rows):
    return pl.BlockSpec((1, rows, ROW_TILE), lambda b, t: (b, 0, t))


def _in0(x, ctx, mods, w_in, g_q, w_uq, g_kv, w_uk, w_uvt, tabs):
    B = x.shape[0]
    outs = (
        jax.ShapeDtypeStruct((B, T_ALL, HEADS_W), jnp.bfloat16),
        jax.ShapeDtypeStruct((B, T_ALL, HEADS_W), jnp.bfloat16),
        jax.ShapeDtypeStruct((B, MLA_HEADS * MLA_VROWS, T_ALL), jnp.bfloat16),
        jax.ShapeDtypeStruct((B, T_ALL, POOL_WIDTH), jnp.float32),
    )
    return pl.pallas_call(
        _in0_kernel,
        out_shape=outs,
        grid=(B, N_TILES),
        in_specs=_two_stream_specs() + [
            _mod_spec(),
            _const_spec(w_in.shape), _const_spec(g_q.shape), _const_spec(w_uq.shape),
            _const_spec(g_kv.shape), _const_spec(w_uk.shape), _const_spec(w_uvt.shape),
            _table_spec(), _table_spec(), _table_spec(),
        ],
        out_specs=(_row_spec(HEADS_W), _row_spec(HEADS_W), _vt_spec(MLA_HEADS * MLA_VROWS),
                   _row_spec(POOL_WIDTH)),
        compiler_params=_params(2),
        name="mla_pool_in_proj",
    )(x, ctx, mods, w_in, g_q, w_uq, g_kv, w_uk, w_uvt, *tabs)


def _in1_kernel(x_ref, mod_ref, w_in_ref, w_vt_ref, gq_ref, gk_ref, c_ref, s_ref, q_ref, k_ref, v_ref):
    m = mod_ref[0, 0]
    h = _bf(_rms(x_ref[0]) * (1.0 + m[1:2]) + m[0:1])
    a = _dot(h, w_in_ref[...])
    c, s = c_ref[...], s_ref[...]
    scale = GQA_HEAD_DIM ** -0.5 * LOG2_E

    def head(j, gain):
        xn = _rms(a[:, j * LANES:(j + 1) * LANES]) * gain
        return xn * c + pltpu.roll(xn, LANES // 2, axis=1) * s

    for j in range(GQA_HEADS):
        q_ref[0, :, j * LANES:(j + 1) * LANES] = _bf(head(j, gq_ref[...]) * scale)
    for j in range(GQA_KV_HEADS):
        k_ref[0, :, j * LANES:(j + 1) * LANES] = _bf(head(GQA_HEADS + j, gk_ref[...]))
    vt = _dot_nt(w_vt_ref[...], h)
    ones = jnp.ones((ONES_ROWS, vt.shape[1]), jnp.bfloat16)
    for j in range(GQA_KV_HEADS):
        v_ref[0, j * GQA_VROWS:j * GQA_VROWS + GQA_HEAD_DIM, :] = _bf(vt[j * GQA_HEAD_DIM:(j + 1) * GQA_HEAD_DIM])
        v_ref[0, j * GQA_VROWS + GQA_HEAD_DIM:(j + 1) * GQA_VROWS, :] = ones


def _in1(xa, mods, w_in, w_vt, g_q, g_k, tabs):
    B = xa.shape[0]
    kv_w = GQA_KV_HEADS * LANES
    outs = (
        jax.ShapeDtypeStruct((B, T_ALL, HEADS_W), jnp.bfloat16),
        jax.ShapeDtypeStruct((B, T_ALL, kv_w), jnp.bfloat16),
        jax.ShapeDtypeStruct((B, GQA_KV_HEADS * GQA_VROWS, T_ALL), jnp.bfloat16),
    )
    return pl.pallas_call(
        _in1_kernel,
        out_shape=outs,
        grid=(B, N_TILES),
        in_specs=[
            _row_spec(D_MODEL), _mod_spec(), _const_spec(w_in.shape), _const_spec(w_vt.shape),
            _const_spec(g_q.shape), _const_spec(g_k.shape), _table_spec(), _table_spec(),
        ],
        out_specs=(_row_spec(HEADS_W), _row_spec(kv_w), _vt_spec(GQA_KV_HEADS * GQA_VROWS)),
        compiler_params=_params(2),
        name="gqa_in_proj",
    )(xa, mods, w_in, w_vt, g_q, g_k, *tabs)


def _score(q, k):
    return _dot_nt(k, q)


def _col_max(s):
    return jnp.max(s, axis=0, keepdims=True)


def _accumulate(s, cm, vt, state):
    m, acc = state
    m_new = jnp.maximum(m, cm)
    p = _bf(jnp.exp2(s - m_new))
    return m_new, jnp.exp2(m - m_new) * acc + _dot(vt, p)


def _attn_kernel(q_ref, k_ref, vt_ref, o_ref, *s_refs, n_heads, shared_kv, dv, with_ctx):
    qi = pl.program_id(2)
    tq = q_ref.shape[1]
    vrows = dv + ONES_ROWS
    heads = range(n_heads)
    n_chunks = T_ALL // KV_CHUNK
    assert n_chunks % 2 == 1
    s0_ref, s1_ref = s_refs[:n_heads], s_refs[n_heads:]

    def q_of(hd):
        return q_ref[0, :, hd * LANES:(hd + 1) * LANES]

    def k_of(hd, rows):
        j = 0 if shared_kv else hd
        return k_ref[0, rows, j * LANES:(j + 1) * LANES]

    def vt_of(hd, rows):
        j = 0 if shared_kv else hd
        return vt_ref[0, j * vrows:(j + 1) * vrows, rows]

    def chunk(c):
        return pl.ds(pl.multiple_of(c * KV_CHUNK, KV_CHUNK), KV_CHUNK)

    def init():
        return tuple((jnp.full((1, tq), -jnp.inf, jnp.float32), jnp.zeros((vrows, tq), jnp.float32))
                     for _ in heads)

    def finish(states):
        ot = jnp.concatenate([acc[:dv] / acc[dv:dv + 1] for _, acc in states], axis=0)
        o_ref[0] = _bf(ot.T)

    def latent_rows():
        qs = [q_of(hd) for hd in heads]

        def park(c, hd, buf):
            s = _score(qs[hd], k_of(hd, chunk(c)))
            buf[hd][...] = s
            return _col_max(s)

        def consume(c, hd, buf, cm, state):
            return _accumulate(buf[hd][...], cm, vt_of(hd, chunk(c)), state)

        def half_step(c, cur, nxt, cms, states):
            new_cms, new_states = [], []
            for hd in heads:
                new_cms.append(park(c + 1, hd, nxt))
                new_states.append(consume(c, hd, cur, cms[hd], states[hd]))
            return tuple(new_cms), tuple(new_states)

        def pair(c, carry):
            carry = half_step(c, s0_ref, s1_ref, *carry)
            return half_step(c + 1, s1_ref, s0_ref, *carry)

        def body(i, carry):
            for k in range(PAIRS_PER_TRIP):
                carry = pair(2 * (PAIRS_PER_TRIP * i + k), carry)
            return carry

        n_pairs = n_chunks // 2
        carry = (tuple(park(0, hd, s0_ref) for hd in heads), init())
        carry = lax.fori_loop(0, n_pairs // PAIRS_PER_TRIP, body, carry)
        for k in range(n_pairs - n_pairs % PAIRS_PER_TRIP, n_pairs):
            carry = pair(2 * k, carry)
        cms, states = carry
        finish([consume(n_chunks - 1, hd, s0_ref, cms[hd], states[hd]) for hd in heads])

    def context_rows():
        rows = pl.ds(SEQ, CTX_LEN)
        states = init()
        out = []
        for hd in heads:
            s = _score(q_of(hd), k_of(hd, rows))
            out.append(_accumulate(s, _col_max(s), vt_of(hd, rows), states[hd]))
        finish(out)

    if with_ctx:
        pl.when(qi < N_LAT_TILES)(latent_rows)
        pl.when(qi == N_LAT_TILES)(context_rows)
    else:
        latent_rows()


def _attention(q, k, vt, *, n_heads, shared_kv, dv, with_ctx):
    B = q.shape[0]
    groups = q.shape[2] // (n_heads * LANES)
    kv_heads = 1 if shared_kv else n_heads
    vrows = dv + ONES_ROWS
    n_q = N_TILES if with_ctx else N_LAT_TILES
    q_spec = pl.BlockSpec((1, ROW_TILE, n_heads * LANES), lambda b, g, i: (b, i, g))
    k_spec = pl.BlockSpec((1, T_ALL, kv_heads * LANES), lambda b, g, i: (b, 0, g))
    vt_spec = pl.BlockSpec((1, kv_heads * vrows, T_ALL), lambda b, g, i: (b, g, 0))
    o_spec = pl.BlockSpec((1, ROW_TILE, n_heads * dv), lambda b, g, i: (b, i, g))
    return pl.pallas_call(
        functools.partial(_attn_kernel, n_heads=n_heads, shared_kv=shared_kv, dv=dv, with_ctx=with_ctx),
        out_shape=jax.ShapeDtypeStruct((B, n_q * ROW_TILE, groups * n_heads * dv), jnp.bfloat16),
        grid=(B, groups, n_q),
        in_specs=[q_spec, k_spec, vt_spec],
        out_specs=o_spec,
        scratch_shapes=[pltpu.VMEM((KV_CHUNK, ROW_TILE), jnp.float32)] * (2 * n_heads),
        compiler_params=_params(3),
        name="attention",
    )(q, k, vt)


def _out0_kernel(x_ref, ctx_ref, mod_ref, o_ref, pp_ref, p_ref, pn_ref, w_pool_ref, s_pool_ref,
                 w_oa_ref, w_op_ref, y_ref):
    t = pl.program_id(1)
    pe, prev_ok, next_ok = _with_halo(pp_ref, p_ref, pn_ref, t)
    pe = jnp.where(_halo_row_mask(prev_ok, next_ok), pe, 0.0)
    t0 = jnp.where(t == N_LAT_TILES, 0, t * ROW_TILE)
    t_len = jnp.where(t == N_LAT_TILES, CTX_LEN, SEQ)
    pos = t0 + lax.broadcasted_iota(jnp.int32, (ROW_TILE, 1), 0)

    pooled = []
    for g, w in enumerate(POOL_WINDOWS):
        run = pe[:, g * LANES:(g + 1) * LANES]
        span = 1
        while span < w:
            run = run + pltpu.roll(run, run.shape[0] - span, axis=0)
            span *= 2
        win = _shift_rows(run, w // 2)
        lo = jnp.clip(pos - w // 2, 0, t_len)
        hi = jnp.clip(pos - w // 2 + w, 0, t_len)
        d = win / (hi - lo).astype(jnp.float32) - p_ref[0, :, g * LANES:(g + 1) * LANES]
        pooled.append(_dot(_bf(d), w_pool_ref[g]))
    pooled = jnp.concatenate(pooled, axis=-1) * s_pool_ref[...]

    y = _dot(o_ref[0], w_oa_ref[...]) + _dot(_bf(pooled), w_op_ref[...])
    y_ref[0] = _two_stream_rows(x_ref, ctx_ref) + mod_ref[0, 0][2:3] * y


def _out0(x, ctx, mods, o, p, w_pool, s_pool, w_oa, w_op):
    B = x.shape[0]
    return pl.pallas_call(
        _out0_kernel,
        out_shape=jax.ShapeDtypeStruct((B, T_ALL, D_MODEL), jnp.float32),
        grid=(B, N_TILES),
        in_specs=_two_stream_specs() + [
            _mod_spec(), _row_spec(o.shape[2]),
            _prev_halo_spec(POOL_WIDTH), _row_spec(POOL_WIDTH), _next_halo_spec(POOL_WIDTH),
            _const_spec(w_pool.shape), _const_spec(s_pool.shape),
            _const_spec(w_oa.shape), _const_spec(w_op.shape),
        ],
        out_specs=_row_spec(D_MODEL),
        compiler_params=_params(2),
        name="mla_pool_out_proj",
    )(x, ctx, mods, o, p, p, p, w_pool, s_pool, w_oa, w_op)


def _out1_kernel(x_ref, mod_ref, o_ref, w_ref, y_ref):
    y_ref[0] = x_ref[0] + mod_ref[0, 0][2:3] * _dot(o_ref[0], w_ref[...])


def _out1(xa, mods, o, w_out):
    B = xa.shape[0]
    return pl.pallas_call(
        _out1_kernel,
        out_shape=jax.ShapeDtypeStruct((B, SEQ, D_MODEL), jnp.float32),
        grid=(B, N_LAT_TILES),
        in_specs=[_row_spec(D_MODEL), _mod_spec(), _row_spec(HEADS_W), _const_spec(w_out.shape)],
        out_specs=_row_spec(D_MODEL),
        compiler_params=_params(2),
        name="gqa_out_proj",
    )(xa, mods, o, w_out)


def _ffn_kernel(xp_ref, x_ref, xn_ref, mod_ref, w_up_ref, cw_ref, cb_ref, w_dn_ref, gf_ref, y_ref,
                *, final_norm):
    t = pl.program_id(1)
    m = mod_ref[0, 0]
    xe, prev_ok, next_ok = _with_halo(xp_ref, x_ref, xn_ref, t)
    he = _rms(xe) * (1.0 + m[4:5]) + m[3:4]
    he = _bf(jnp.where(_halo_row_mask(prev_ok, next_ok), he, 0.0))
    hm = he[SUBLANES:SUBLANES + ROW_TILE]

    def up(cs):
        g = _dot(he, w_up_ref[:, cs])
        u = _dot(hm, w_up_ref[:, D_FF + cs.start:D_FF + cs.stop])
        return g, u

    def down(cs, g, u):
        acc = cb_ref[:, cs] + _shift_rows(g, 1) * cw_ref[0:1, cs]
        acc = acc + g[SUBLANES:SUBLANES + ROW_TILE] * cw_ref[1:2, cs]
        acc = acc + _shift_rows(g, -1) * cw_ref[2:3, cs]
        return _dot(_bf(jax.nn.silu(acc) * u), w_dn_ref[cs, :])

    chunks = [slice(c0, c0 + FF_CHUNK) for c0 in range(0, D_FF, FF_CHUNK)]
    y = jnp.zeros((ROW_TILE, D_MODEL), jnp.float32)
    pending = up(chunks[0])
    for cs, cs_next in zip(chunks, chunks[1:] + [None]):
        ahead = up(cs_next) if cs_next is not None else None
        y = y + down(cs, *pending)
        pending = ahead
    out = x_ref[0] + m[5:6] * y
    if final_norm:
        out = _rms(out) * gf_ref[...]
    y_ref[0] = out


def _ffn(xa, mods, w_up, conv_w, conv_b, w_dn, g_final, *, final_norm):
    B = xa.shape[0]
    rows = xa.shape[1]
    n_t = rows // ROW_TILE
    assert final_norm == (rows == SEQ)
    return pl.pallas_call(
        functools.partial(_ffn_kernel, final_norm=final_norm),
        out_shape=jax.ShapeDtypeStruct((B, rows, D_MODEL), jnp.float32),
        grid=(B, n_t),
        in_specs=[
            _prev_halo_spec(D_MODEL), _row_spec(D_MODEL), _next_halo_spec(D_MODEL, rows), _mod_spec(),
            _const_spec(w_up.shape), _const_spec(conv_w.shape), _const_spec(conv_b.shape),
            _const_spec(w_dn.shape), _const_spec(g_final.shape),
        ],
        out_specs=_row_spec(D_MODEL),
        compiler_params=_params(2),
        name="conv_ffn",
    )(xa, xa, xa, mods, w_up, conv_w, conv_b, w_dn, g_final)


def _rope_angles(rope_dim):
    rows = SEQ // GRID_W
    row = jnp.repeat(jnp.arange(rows, dtype=jnp.float32), GRID_W)
    col = jnp.tile(jnp.arange(GRID_W, dtype=jnp.float32), rows)
    n_freq = rope_dim // 4
    freq = ROPE_THETA ** (-jnp.arange(n_freq, dtype=jnp.float32) / n_freq)
    ang = jnp.concatenate([row[:, None] * freq, col[:, None] * freq], axis=-1)
    return jnp.concatenate([ang, jnp.zeros((CTX_LEN, rope_dim // 2), jnp.float32)], axis=0)


def _mla_tables():
    ang = _rope_angles(MLA_ROPE)
    cos, sin = jnp.cos(ang), jnp.sin(ang)
    half = MLA_ROPE // 2
    ones = jnp.ones((T_ALL, MLA_NOPE), jnp.float32)
    zn = jnp.zeros((T_ALL, MLA_NOPE), jnp.float32)
    zh = jnp.zeros((T_ALL, half), jnp.float32)
    zt = jnp.zeros((T_ALL, LANES - MLA_NOPE - MLA_ROPE), jnp.float32)
    c = jnp.concatenate([ones, cos, cos, zt], axis=-1)
    s_up = jnp.concatenate([zn, zh, sin, zt], axis=-1)
    s_dn = jnp.concatenate([zn, -sin, zh, zt], axis=-1)
    return c, s_up, s_dn


def _gqa_tables():
    ang = _rope_angles(GQA_HEAD_DIM)
    cos, sin = jnp.cos(ang), jnp.sin(ang)
    return jnp.concatenate([cos, cos], axis=-1), jnp.concatenate([-sin, sin], axis=-1)


def _mla_head_cols(w, n_heads, nope, rope):
    K = w.shape[0]
    w = w.reshape(K, n_heads, nope + rope)
    parts = [w[..., :nope]]
    if rope:
        parts += [w[..., nope::2], w[..., nope + 1::2]]
    parts.append(jnp.zeros((K, n_heads, LANES - nope - rope), w.dtype))
    return jnp.concatenate(parts, axis=-1).reshape(K, n_heads * LANES)


def _deinterleave(n):
    return np.concatenate([np.arange(0, n, 2), np.arange(1, n, 2)])


def kernel(x, c, ctx, c_ctx, w_mod, b_mod, mix0_w_in, mla_g_q, mla_w_uq, mla_g_kv, mla_w_uk, mla_w_uv,
           pool_w, pool_scale, mix0_w_out, gqa_w_in, gqa_g_q, gqa_g_k, gqa_w_out,
           ffn_w_up, ffn_conv_w, ffn_conv_b, ffn_w_down, g_final):
    B = x.shape[0]
    assert x.shape == (B, SEQ, D_MODEL) and ctx.shape == (B, CTX_LEN, D_MODEL) and B < SUBLANES
    assert w_mod.shape[0] == 2

    cc = jnp.zeros((SUBLANES, D_MODEL), jnp.float32).at[:B].set(c).at[B].set(c_ctx)
    mod_all = _modulation(cc, w_mod, b_mod).reshape(2, SUBLANES, N_MOD, D_MODEL)

    def mods_of(i):
        lat = mod_all[i, :B]
        con = jnp.broadcast_to(mod_all[i, B], lat.shape)
        return jnp.stack([lat, con], axis=1)

    w_in = mix0_w_in[0]
    i_kv = MLA_Q_RANK
    i_kr = i_kv + MLA_KV_RANK
    i_p = i_kr + MLA_ROPE
    w_kr = _mla_head_cols(w_in[:, i_kr:i_p], 1, 0, MLA_ROPE)
    w_kr = jnp.roll(w_kr, MLA_NOPE, axis=1)
    w_in0 = _bf(jnp.concatenate([w_in[:, :i_kr], w_kr, w_in[:, i_p:]], axis=1))
    w_uq = _bf(_mla_head_cols(mla_w_uq[0], MLA_HEADS, MLA_NOPE, MLA_ROPE))
    w_uk = _bf(_mla_head_cols(mla_w_uk[0], MLA_HEADS, MLA_NOPE, 0))
    w_uvt = _bf(mla_w_uv[0].T)
    w_out0 = mix0_w_out[0]
    w_oa = _bf(w_out0[:MLA_HEADS * MLA_V])
    w_op = _bf(w_out0[MLA_HEADS * MLA_V:])

    m0 = mods_of(0)
    q, k, vt, p = _in0(x, ctx, m0, w_in0, mla_g_q[0][None], w_uq, mla_g_kv[0][None], w_uk, w_uvt,
                       _mla_tables())
    o = _attention(q, k, vt, n_heads=MLA_HEADS_PER_STEP, shared_kv=False, dv=MLA_V, with_ctx=True)
    xa = _out0(x, ctx, m0, o, p, _bf(pool_w[0]), pool_scale[0][None], w_oa, w_op)
    xa = _ffn(xa, m0, _bf(ffn_w_up[0]), ffn_conv_w[0], ffn_conv_b[0][None], _bf(ffn_w_down[0]),
              g_final[None], final_norm=False)

    perm = _deinterleave(GQA_HEAD_DIM)
    n_qk = GQA_HEADS + GQA_KV_HEADS
    w_in = gqa_w_in[0]
    w_qk = w_in[:, :n_qk * LANES].reshape(D_MODEL, n_qk, LANES)[:, :, perm].reshape(D_MODEL, n_qk * LANES)
    w_vt = _bf(w_in[:, n_qk * LANES:].T)
    m1 = mods_of(1)
    q, k, vt = _in1(xa, m1, _bf(w_qk), w_vt, gqa_g_q[0][perm][None], gqa_g_k[0][perm][None], _gqa_tables())
    o = _attention(q, k, vt, n_heads=GQA_GROUP, shared_kv=True, dv=GQA_HEAD_DIM, with_ctx=False)
    xa = _out1(xa, m1, o, _bf(gqa_w_out[0]))
    return _ffn(xa, m1, _bf(ffn_w_up[1]), ffn_conv_w[1], ffn_conv_b[1][None], _bf(ffn_w_down[1]),
                g_final[None], final_norm=True)
```

```python
import functools

import jax
import jax.numpy as jnp
import numpy as np
from jax import lax
from jax.experimental import pallas as pl
from jax.experimental.pallas import tpu as pltpu

D_MODEL = 1024
SEQ = 8192
GRID_W = 64
CTX_LEN = 256
T_ALL = SEQ + CTX_LEN
ROPE_THETA = 10000.0
EPS = 1e-6
N_MOD = 6
MLA_HEADS = 8
MLA_NOPE = 64
MLA_ROPE = 32
MLA_V = 64
MLA_Q_RANK = 384
MLA_KV_RANK = 256
POOL_WINDOWS = (2, 4, 8, 16)
POOL_WIDTH = 512
GQA_HEADS = 8
GQA_KV_HEADS = 2
GQA_HEAD_DIM = 128
GQA_GROUP = GQA_HEADS // GQA_KV_HEADS
D_FF = 2816
CONV_W = 3

LANES = 128
SUBLANES = 8
ROW_TILE = 256
N_LAT_TILES = SEQ // ROW_TILE
N_TILES = T_ALL // ROW_TILE
HALO_BLOCKS_PER_TILE = ROW_TILE // SUBLANES
KV_CHUNK = 768
PAIRS_PER_TRIP = 2
TILES_PER_STEP = 4
FF_CHUNK = 256
VMEM_LIMIT = 56 * 1024 * 1024

MIX0_IN_PAD = MLA_Q_RANK + MLA_KV_RANK + LANES + POOL_WIDTH
HEADS_W = MLA_HEADS * LANES
LOG2_E = 1.4426950408889634
ONES_ROWS = 16
MLA_VROWS = MLA_V + ONES_ROWS
GQA_VROWS = GQA_HEAD_DIM + ONES_ROWS
MLA_HEADS_PER_STEP = 4


def _params(n_axes):
    return pltpu.CompilerParams(dimension_semantics=("arbitrary",) * n_axes, vmem_limit_bytes=VMEM_LIMIT)


def _rms(x):
    return x * lax.rsqrt(jnp.mean(x * x, axis=-1, keepdims=True) + EPS)


def _dot(a, b):
    return jnp.dot(a, b, preferred_element_type=jnp.float32)


def _dot_nt(a, b):
    return lax.dot_general(a, b, (((1,), (1,)), ((), ())), preferred_element_type=jnp.float32)


def _bf(x):
    return x.astype(jnp.bfloat16)


def _mod_kernel(c_ref, w_ref, b_ref, o_ref):
    s = jax.nn.silu(c_ref[...])
    o_ref[0] = jnp.dot(s, w_ref[0], preferred_element_type=jnp.float32,
                       precision=lax.Precision.HIGHEST) + b_ref[0]


def _modulation(cc, w_mod, b_mod):
    depth = w_mod.shape[0]
    return pl.pallas_call(
        _mod_kernel,
        out_shape=jax.ShapeDtypeStruct((depth, SUBLANES, N_MOD * D_MODEL), jnp.float32),
        grid=(depth, N_MOD),
        in_specs=[
            pl.BlockSpec((SUBLANES, D_MODEL), lambda i, j: (0, 0)),
            pl.BlockSpec((1, D_MODEL, D_MODEL), lambda i, j: (i, 0, j)),
            pl.BlockSpec((1, 1, D_MODEL), lambda i, j: (i, 0, j)),
        ],
        out_specs=pl.BlockSpec((1, SUBLANES, D_MODEL), lambda i, j: (i, 0, j)),
        compiler_params=_params(2),
        name="modulation",
    )(cc, w_mod, b_mod.reshape(depth, 1, N_MOD * D_MODEL))


def _row_spec(width, rows=ROW_TILE):
    return pl.BlockSpec((1, rows, width), lambda b, t: (b, t, 0))


def _mod_spec():
    return pl.BlockSpec((1, 1, N_MOD, D_MODEL), lambda b, t: (b, t // N_LAT_TILES, 0, 0))


def _const_spec(shape):
    nd = len(shape)
    return pl.BlockSpec(shape, lambda b, t: (0,) * nd)


def _table_spec():
    return pl.BlockSpec((ROW_TILE, LANES), lambda b, t: (t, 0))


def _prev_halo_spec(width):
    return pl.BlockSpec((1, SUBLANES, width),
                        lambda b, t: (b, jnp.maximum(t * HALO_BLOCKS_PER_TILE - 1, 0), 0))


def _next_halo_spec(width, total_rows=T_ALL):
    last = total_rows // SUBLANES - 1
    return pl.BlockSpec((1, SUBLANES, width),
                        lambda b, t: (b, jnp.minimum((t + 1) * HALO_BLOCKS_PER_TILE, last), 0))


def _halo_valid(t):
    prev_ok = jnp.logical_and(t != 0, t != N_LAT_TILES)
    next_ok = jnp.logical_and(t != N_LAT_TILES - 1, t != N_LAT_TILES)
    return prev_ok, next_ok


def _with_halo(prev_ref, main_ref, next_ref, t):
    prev_ok, next_ok = _halo_valid(t)
    return jnp.concatenate([prev_ref[0], main_ref[0], next_ref[0]], axis=0), prev_ok, next_ok


def _halo_row_mask(prev_ok, next_ok):
    r = lax.broadcasted_iota(jnp.int32, (ROW_TILE + 2 * SUBLANES, 1), 0)
    return jnp.logical_and(jnp.logical_or(r >= SUBLANES, prev_ok),
                           jnp.logical_or(r < ROW_TILE + SUBLANES, next_ok))


def _shift_rows(x, k):
    n = x.shape[0]
    return pltpu.roll(x, k % n, axis=0)[SUBLANES:SUBLANES + ROW_TILE]


def _rope_mla(x, c, s_up, s_dn):
    return x * c + pltpu.roll(x, MLA_ROPE // 2, axis=1) * s_up + pltpu.roll(x, LANES - MLA_ROPE // 2, axis=1) * s_dn


def _two_stream_specs(width=D_MODEL):
    return [pl.BlockSpec((1, ROW_TILE, width), lambda b, t: (b, jnp.minimum(t, N_LAT_TILES - 1), 0)),
            pl.BlockSpec((1, CTX_LEN, width), lambda b, t: (b, 0, 0))]


def _two_stream_rows(x_ref, ctx_ref):
    return jnp.where(pl.program_id(1) == N_LAT_TILES, ctx_ref[0], x_ref[0])


def _in0_kernel(x_ref, ctx_ref, mod_ref, w_in_ref, gq_ref, w_uq_ref, gkv_ref, w_uk_ref, w_uvt_ref,
                c_ref, su_ref, sd_ref, q_ref, k_ref, v_ref, p_ref):
    m = mod_ref[0, 0]
    h = _bf(_rms(_two_stream_rows(x_ref, ctx_ref)) * (1.0 + m[1:2]) + m[0:1])
    a = _dot(h, w_in_ref[...])
    i_kv = MLA_Q_RANK
    i_kr = i_kv + MLA_KV_RANK
    i_p = i_kr + LANES
    c, su, sd = c_ref[...], su_ref[...], sd_ref[...]
    scale = (MLA_NOPE + MLA_ROPE) ** -0.5 * LOG2_E

    q = _dot(_bf(_rms(a[:, :i_kv]) * gq_ref[...]), w_uq_ref[...])
    ckv = _bf(_rms(a[:, i_kv:i_kr]) * gkv_ref[...])
    kn = _dot(ckv, w_uk_ref[...])
    kr = _rope_mla(a[:, i_kr:i_p], c, su, sd)
    vt = _dot_nt(w_uvt_ref[...], ckv)
    ones = jnp.ones((ONES_ROWS, vt.shape[1]), jnp.bfloat16)
    for hd in range(MLA_HEADS):
        sl = slice(hd * LANES, (hd + 1) * LANES)
        q_ref[0, :, sl] = _bf(_rope_mla(q[:, sl], c, su, sd) * scale)
        k_ref[0, :, sl] = _bf(kn[:, sl] + kr)
        v_ref[0, hd * MLA_VROWS:hd * MLA_VROWS + MLA_V, :] = _bf(vt[hd * MLA_V:(hd + 1) * MLA_V])
        v_ref[0, hd * MLA_VROWS + MLA_V:(hd + 1) * MLA_VROWS, :] = ones
    p_ref[0] = a[:, i_p:]


def _vt_spec(rows):
    return pl.BlockSpec((1, rows, ROW_TILE), lambda b, t: (b, 0, t))


def _in0(x, ctx, mods, w_in, g_q, w_uq, g_kv, w_uk, w_uvt, tabs):
    B = x.shape[0]
    outs = (
        jax.ShapeDtypeStruct((B, T_ALL, HEADS_W), jnp.bfloat16),
        jax.ShapeDtypeStruct((B, T_ALL, HEADS_W), jnp.bfloat16),
        jax.ShapeDtypeStruct((B, MLA_HEADS * MLA_VROWS, T_ALL), jnp.bfloat16),
        jax.ShapeDtypeStruct((B, T_ALL, POOL_WIDTH), jnp.float32),
    )
    return pl.pallas_call(
        _in0_kernel,
        out_shape=outs,
        grid=(B, N_TILES),
        in_specs=_two_stream_specs() + [
            _mod_spec(),
            _const_spec(w_in.shape), _const_spec(g_q.shape), _const_spec(w_uq.shape),
            _const_spec(g_kv.shape), _const_spec(w_uk.shape), _const_spec(w_uvt.shape),
            _table_spec(), _table_spec(), _table_spec(),
        ],
        out_specs=(_row_spec(HEADS_W), _row_spec(HEADS_W), _vt_spec(MLA_HEADS * MLA_VROWS),
                   _row_spec(POOL_WIDTH)),
        compiler_params=_params(2),
        name="mla_pool_in_proj",
    )(x, ctx, mods, w_in, g_q, w_uq, g_kv, w_uk, w_uvt, *tabs)


def _in1_kernel(x_ref, mod_ref, w_in_ref, w_vt_ref, gq_ref, gk_ref, c_ref, s_ref, q_ref, k_ref, v_ref):
    m = mod_ref[0, 0]
    h = _bf(_rms(x_ref[0]) * (1.0 + m[1:2]) + m[0:1])
    a = _dot(h, w_in_ref[...])
    c, s = c_ref[...], s_ref[...]
    scale = GQA_HEAD_DIM ** -0.5 * LOG2_E

    def head(j, gain):
        xn = _rms(a[:, j * LANES:(j + 1) * LANES]) * gain
        return xn * c + pltpu.roll(xn, LANES // 2, axis=1) * s

    for j in range(GQA_HEADS):
        q_ref[0, :, j * LANES:(j + 1) * LANES] = _bf(head(j, gq_ref[...]) * scale)
    for j in range(GQA_KV_HEADS):
        k_ref[0, :, j * LANES:(j + 1) * LANES] = _bf(head(GQA_HEADS + j, gk_ref[...]))
    vt = _dot_nt(w_vt_ref[...], h)
    ones = jnp.ones((ONES_ROWS, vt.shape[1]), jnp.bfloat16)
    for j in range(GQA_KV_HEADS):
        v_ref[0, j * GQA_VROWS:j * GQA_VROWS + GQA_HEAD_DIM, :] = _bf(vt[j * GQA_HEAD_DIM:(j + 1) * GQA_HEAD_DIM])
        v_ref[0, j * GQA_VROWS + GQA_HEAD_DIM:(j + 1) * GQA_VROWS, :] = ones


def _in1(xa, mods, w_in, w_vt, g_q, g_k, tabs):
    B = xa.shape[0]
    kv_w = GQA_KV_HEADS * LANES
    outs = (
        jax.ShapeDtypeStruct((B, T_ALL, HEADS_W), jnp.bfloat16),
        jax.ShapeDtypeStruct((B, T_ALL, kv_w), jnp.bfloat16),
        jax.ShapeDtypeStruct((B, GQA_KV_HEADS * GQA_VROWS, T_ALL), jnp.bfloat16),
    )
    return pl.pallas_call(
        _in1_kernel,
        out_shape=outs,
        grid=(B, N_TILES),
        in_specs=[
            _row_spec(D_MODEL), _mod_spec(), _const_spec(w_in.shape), _const_spec(w_vt.shape),
            _const_spec(g_q.shape), _const_spec(g_k.shape), _table_spec(), _table_spec(),
        ],
        out_specs=(_row_spec(HEADS_W), _row_spec(kv_w), _vt_spec(GQA_KV_HEADS * GQA_VROWS)),
        compiler_params=_params(2),
        name="gqa_in_proj",
    )(xa, mods, w_in, w_vt, g_q, g_k, *tabs)


def _score(q, k):
    return _dot_nt(k, q)


def _col_max(s):
    return jnp.max(s, axis=0, keepdims=True)


def _accumulate(s, cm, vt, state):
    m, acc = state
    m_new = jnp.maximum(m, cm)
    p = _bf(jnp.exp2(s - m_new))
    return m_new, jnp.exp2(m - m_new) * acc + _dot(vt, p)


def _attn_views(k_ref, vt_ref, *, shared_kv, vrows):
    def k_of(hd, rows):
        j = 0 if shared_kv else hd
        return k_ref[0, rows, j * LANES:(j + 1) * LANES]

    def vt_of(hd, rows):
        j = 0 if shared_kv else hd
        return vt_ref[0, j * vrows:(j + 1) * vrows, rows]

    return k_of, vt_of


def _attn_init(n_heads, vrows, tq):
    return tuple((jnp.full((1, tq), -jnp.inf, jnp.float32), jnp.zeros((vrows, tq), jnp.float32))
                 for _ in range(n_heads))


def _attn_output(states, dv):
    ot = jnp.concatenate([acc[:dv] / acc[dv:dv + 1] for _, acc in states], axis=0)
    return _bf(ot.T)


def _attn_kernel(q_ref, k_ref, vt_ref, o_ref, *s_refs, n_heads, shared_kv, dv):
    vrows = dv + ONES_ROWS
    heads = range(n_heads)
    n_chunks = T_ALL // KV_CHUNK
    assert n_chunks % 2 == 1 and (n_chunks - 3) % (2 * PAIRS_PER_TRIP) == 0
    k_of, vt_of = _attn_views(k_ref, vt_ref, shared_kv=shared_kv, vrows=vrows)

    def chunk(c):
        return pl.ds(pl.multiple_of(c * KV_CHUNK, KV_CHUNK), KV_CHUNK)

    def qs_of(tile):
        return [q_ref[0, tile * ROW_TILE:(tile + 1) * ROW_TILE, hd * LANES:(hd + 1) * LANES] for hd in heads]

    def park(qs, c, hd, buf):
        s = _score(qs[hd], k_of(hd, chunk(c)))
        buf[hd][...] = s
        return _col_max(s)

    def consume(c, hd, buf, cm, state):
        return _accumulate(buf[hd][...], cm, vt_of(hd, chunk(c)), state)

    def half_step(qs, c_next, c, cur, nxt, cms, states):
        new_cms, new_states = [], []
        for hd in heads:
            new_cms.append(park(qs, c_next, hd, nxt))
            new_states.append(consume(c, hd, cur, cms[hd], states[hd]))
        return tuple(new_cms), tuple(new_states)

    even, odd = s_refs[:n_heads], s_refs[n_heads:]
    cms = tuple(park(qs_of(0), 0, hd, even) for hd in heads)
    for tile in range(TILES_PER_STEP):
        qs = qs_of(tile)

        def pair(c, carry, qs=qs, even=even, odd=odd):
            carry = half_step(qs, c + 1, c, odd, even, *carry)
            return half_step(qs, c + 2, c + 1, even, odd, *carry)

        def body(i, carry, pair=pair):
            for k in range(PAIRS_PER_TRIP):
                carry = pair(1 + 2 * (PAIRS_PER_TRIP * i + k), carry)
            return carry

        carry = half_step(qs, 1, 0, even, odd, cms, _attn_init(n_heads, vrows, ROW_TILE))
        carry = lax.fori_loop(0, (n_chunks - 3) // (2 * PAIRS_PER_TRIP), body, carry)
        cms, states = half_step(qs, n_chunks - 1, n_chunks - 2, odd, even, *carry)
        if tile + 1 < TILES_PER_STEP:
            cms, states = half_step(qs_of(tile + 1), 0, n_chunks - 1, even, odd, cms, states)
            even, odd = odd, even
        else:
            states = [consume(n_chunks - 1, hd, even, cms[hd], states[hd]) for hd in heads]
        o_ref[0, tile * ROW_TILE:(tile + 1) * ROW_TILE, :] = _attn_output(states, dv)


def _ctx_attn_kernel(q_ref, k_ref, vt_ref, o_ref, *, n_heads, shared_kv, dv):
    vrows = dv + ONES_ROWS
    k_of, vt_of = _attn_views(k_ref, vt_ref, shared_kv=shared_kv, vrows=vrows)
    rows = slice(None)
    states = []
    for hd, state in enumerate(_attn_init(n_heads, vrows, CTX_LEN)):
        s = _score(q_ref[0, :, hd * LANES:(hd + 1) * LANES], k_of(hd, rows))
        states.append(_accumulate(s, _col_max(s), vt_of(hd, rows), state))
    o_ref[0] = _attn_output(states, dv)


def _attention(q, k, vt, *, n_heads, shared_kv, dv, context):
    B = q.shape[0]
    groups = q.shape[2] // (n_heads * LANES)
    kv_heads = 1 if shared_kv else n_heads
    vrows = dv + ONES_ROWS
    params = dict(n_heads=n_heads, shared_kv=shared_kv, dv=dv)
    if context:
        last = T_ALL // CTX_LEN - 1
        return pl.pallas_call(
            functools.partial(_ctx_attn_kernel, **params),
            out_shape=jax.ShapeDtypeStruct((B, CTX_LEN, groups * n_heads * dv), jnp.bfloat16),
            grid=(B, groups),
            in_specs=[pl.BlockSpec((1, CTX_LEN, n_heads * LANES), lambda b, g: (b, last, g)),
                      pl.BlockSpec((1, CTX_LEN, kv_heads * LANES), lambda b, g: (b, last, g)),
                      pl.BlockSpec((1, kv_heads * vrows, CTX_LEN), lambda b, g: (b, g, last))],
            out_specs=pl.BlockSpec((1, CTX_LEN, n_heads * dv), lambda b, g: (b, 0, g)),
            compiler_params=_params(2),
            name="context_attention",
        )(q, k, vt)
    rows = TILES_PER_STEP * ROW_TILE
    return pl.pallas_call(
        functools.partial(_attn_kernel, **params),
        out_shape=jax.ShapeDtypeStruct((B, SEQ, groups * n_heads * dv), jnp.bfloat16),
        grid=(B, groups, SEQ // rows),
        in_specs=[pl.BlockSpec((1, rows, n_heads * LANES), lambda b, g, i: (b, i, g)),
                  pl.BlockSpec((1, T_ALL, kv_heads * LANES), lambda b, g, i: (b, 0, g)),
                  pl.BlockSpec((1, kv_heads * vrows, T_ALL), lambda b, g, i: (b, g, 0))],
        out_specs=pl.BlockSpec((1, rows, n_heads * dv), lambda b, g, i: (b, i, g)),
        scratch_shapes=[pltpu.VMEM((KV_CHUNK, ROW_TILE), jnp.float32)] * (2 * n_heads),
        compiler_params=_params(3),
        name="attention",
    )(q, k, vt)


def _out0_kernel(x_ref, ctx_ref, mod_ref, o_ref, oc_ref, pp_ref, p_ref, pn_ref, w_pool_ref, s_pool_ref,
                 w_oa_ref, w_op_ref, y_ref):
    t = pl.program_id(1)
    pe, prev_ok, next_ok = _with_halo(pp_ref, p_ref, pn_ref, t)
    pe = jnp.where(_halo_row_mask(prev_ok, next_ok), pe, 0.0)
    t0 = jnp.where(t == N_LAT_TILES, 0, t * ROW_TILE)
    t_len = jnp.where(t == N_LAT_TILES, CTX_LEN, SEQ)
    pos = t0 + lax.broadcasted_iota(jnp.int32, (ROW_TILE, 1), 0)

    pooled = []
    for g, w in enumerate(POOL_WINDOWS):
        run = pe[:, g * LANES:(g + 1) * LANES]
        span = 1
        while span < w:
            run = run + pltpu.roll(run, run.shape[0] - span, axis=0)
            span *= 2
        win = _shift_rows(run, w // 2)
        lo = jnp.clip(pos - w // 2, 0, t_len)
        hi = jnp.clip(pos - w // 2 + w, 0, t_len)
        d = win / (hi - lo).astype(jnp.float32) - p_ref[0, :, g * LANES:(g + 1) * LANES]
        pooled.append(_dot(_bf(d), w_pool_ref[g]))
    pooled = jnp.concatenate(pooled, axis=-1) * s_pool_ref[...]

    y = _dot(_two_stream_rows(o_ref, oc_ref), w_oa_ref[...]) + _dot(_bf(pooled), w_op_ref[...])
    y_ref[0] = _two_stream_rows(x_ref, ctx_ref) + mod_ref[0, 0][2:3] * y


def _out0(x, ctx, mods, o, o_ctx, p, w_pool, s_pool, w_oa, w_op):
    B = x.shape[0]
    return pl.pallas_call(
        _out0_kernel,
        out_shape=jax.ShapeDtypeStruct((B, T_ALL, D_MODEL), jnp.float32),
        grid=(B, N_TILES),
        in_specs=_two_stream_specs() + [_mod_spec()] + _two_stream_specs(o.shape[2]) + [
            _prev_halo_spec(POOL_WIDTH), _row_spec(POOL_WIDTH), _next_halo_spec(POOL_WIDTH),
            _const_spec(w_pool.shape), _const_spec(s_pool.shape),
            _const_spec(w_oa.shape), _const_spec(w_op.shape),
        ],
        out_specs=_row_spec(D_MODEL),
        compiler_params=_params(2),
        name="mla_pool_out_proj",
    )(x, ctx, mods, o, o_ctx, p, p, p, w_pool, s_pool, w_oa, w_op)


def _out1_kernel(x_ref, mod_ref, o_ref, w_ref, y_ref):
    y_ref[0] = x_ref[0] + mod_ref[0, 0][2:3] * _dot(o_ref[0], w_ref[...])


def _out1(xa, mods, o, w_out):
    B = xa.shape[0]
    return pl.pallas_call(
        _out1_kernel,
        out_shape=jax.ShapeDtypeStruct((B, SEQ, D_MODEL), jnp.float32),
        grid=(B, N_LAT_TILES),
        in_specs=[_row_spec(D_MODEL), _mod_spec(), _row_spec(HEADS_W), _const_spec(w_out.shape)],
        out_specs=_row_spec(D_MODEL),
        compiler_params=_params(2),
        name="gqa_out_proj",
    )(xa, mods, o, w_out)


def _ffn_kernel(xp_ref, x_ref, xn_ref, mod_ref, w_up_ref, cw_ref, cb_ref, w_dn_ref, gf_ref, y_ref,
                *, final_norm):
    t = pl.program_id(1)
    m = mod_ref[0, 0]
    xe, prev_ok, next_ok = _with_halo(xp_ref, x_ref, xn_ref, t)
    he = _rms(xe) * (1.0 + m[4:5]) + m[3:4]
    he = _bf(jnp.where(_halo_row_mask(prev_ok, next_ok), he, 0.0))
    hm = he[SUBLANES:SUBLANES + ROW_TILE]

    def up(cs):
        g = _dot(he, w_up_ref[:, cs])
        u = _dot(hm, w_up_ref[:, D_FF + cs.start:D_FF + cs.stop])
        return g, u

    def down(cs, g, u):
        acc = cb_ref[:, cs] + _shift_rows(g, 1) * cw_ref[0:1, cs]
        acc = acc + g[SUBLANES:SUBLANES + ROW_TILE] * cw_ref[1:2, cs]
        acc = acc + _shift_rows(g, -1) * cw_ref[2:3, cs]
        return _dot(_bf(jax.nn.silu(acc) * u), w_dn_ref[cs, :])

    chunks = [slice(c0, c0 + FF_CHUNK) for c0 in range(0, D_FF, FF_CHUNK)]
    y = jnp.zeros((ROW_TILE, D_MODEL), jnp.float32)
    pending = up(chunks[0])
    for cs, cs_next in zip(chunks, chunks[1:] + [None]):
        ahead = up(cs_next) if cs_next is not None else None
        y = y + down(cs, *pending)
        pending = ahead
    out = x_ref[0] + m[5:6] * y
    if final_norm:
        out = _rms(out) * gf_ref[...]
    y_ref[0] = out


def _ffn(xa, mods, w_up, conv_w, conv_b, w_dn, g_final, *, final_norm):
    B = xa.shape[0]
    rows = xa.shape[1]
    n_t = rows // ROW_TILE
    assert final_norm == (rows == SEQ)
    return pl.pallas_call(
        functools.partial(_ffn_kernel, final_norm=final_norm),
        out_shape=jax.ShapeDtypeStruct((B, rows, D_MODEL), jnp.float32),
        grid=(B, n_t),
        in_specs=[
            _prev_halo_spec(D_MODEL), _row_spec(D_MODEL), _next_halo_spec(D_MODEL, rows), _mod_spec(),
            _const_spec(w_up.shape), _const_spec(conv_w.shape), _const_spec(conv_b.shape),
            _const_spec(w_dn.shape), _const_spec(g_final.shape),
        ],
        out_specs=_row_spec(D_MODEL),
        compiler_params=_params(2),
        name="conv_ffn",
    )(xa, xa, xa, mods, w_up, conv_w, conv_b, w_dn, g_final)


def _rope_angles(rope_dim):
    rows = SEQ // GRID_W
    row = jnp.repeat(jnp.arange(rows, dtype=jnp.float32), GRID_W)
    col = jnp.tile(jnp.arange(GRID_W, dtype=jnp.float32), rows)
    n_freq = rope_dim // 4
    freq = ROPE_THETA ** (-jnp.arange(n_freq, dtype=jnp.float32) / n_freq)
    ang = jnp.concatenate([row[:, None] * freq, col[:, None] * freq], axis=-1)
    return jnp.concatenate([ang, jnp.zeros((CTX_LEN, rope_dim // 2), jnp.float32)], axis=0)


def _mla_tables():
    ang = _rope_angles(MLA_ROPE)
    cos, sin = jnp.cos(ang), jnp.sin(ang)
    half = MLA_ROPE // 2
    ones = jnp.ones((T_ALL, MLA_NOPE), jnp.float32)
    zn = jnp.zeros((T_ALL, MLA_NOPE), jnp.float32)
    zh = jnp.zeros((T_ALL, half), jnp.float32)
    zt = jnp.zeros((T_ALL, LANES - MLA_NOPE - MLA_ROPE), jnp.float32)
    c = jnp.concatenate([ones, cos, cos, zt], axis=-1)
    s_up = jnp.concatenate([zn, zh, sin, zt], axis=-1)
    s_dn = jnp.concatenate([zn, -sin, zh, zt], axis=-1)
    return c, s_up, s_dn


def _gqa_tables():
    ang = _rope_angles(GQA_HEAD_DIM)
    cos, sin = jnp.cos(ang), jnp.sin(ang)
    return jnp.concatenate([cos, cos], axis=-1), jnp.concatenate([-sin, sin], axis=-1)


def _mla_head_cols(w, n_heads, nope, rope):
    K = w.shape[0]
    w = w.reshape(K, n_heads, nope + rope)
    parts = [w[..., :nope]]
    if rope:
        parts += [w[..., nope::2], w[..., nope + 1::2]]
    parts.append(jnp.zeros((K, n_heads, LANES - nope - rope), w.dtype))
    return jnp.concatenate(parts, axis=-1).reshape(K, n_heads * LANES)


def _deinterleave(n):
    return np.concatenate([np.arange(0, n, 2), np.arange(1, n, 2)])


def kernel(x, c, ctx, c_ctx, w_mod, b_mod, mix0_w_in, mla_g_q, mla_w_uq, mla_g_kv, mla_w_uk, mla_w_uv,
           pool_w, pool_scale, mix0_w_out, gqa_w_in, gqa_g_q, gqa_g_k, gqa_w_out,
           ffn_w_up, ffn_conv_w, ffn_conv_b, ffn_w_down, g_final):
    B = x.shape[0]
    assert x.shape == (B, SEQ, D_MODEL) and ctx.shape == (B, CTX_LEN, D_MODEL) and B < SUBLANES
    assert w_mod.shape[0] == 2

    cc = jnp.zeros((SUBLANES, D_MODEL), jnp.float32).at[:B].set(c).at[B].set(c_ctx)
    mod_all = _modulation(cc, w_mod, b_mod).reshape(2, SUBLANES, N_MOD, D_MODEL)

    def mods_of(i):
        lat = mod_all[i, :B]
        con = jnp.broadcast_to(mod_all[i, B], lat.shape)
        return jnp.stack([lat, con], axis=1)

    w_in = mix0_w_in[0]
    i_kv = MLA_Q_RANK
    i_kr = i_kv + MLA_KV_RANK
    i_p = i_kr + MLA_ROPE
    w_kr = _mla_head_cols(w_in[:, i_kr:i_p], 1, 0, MLA_ROPE)
    w_kr = jnp.roll(w_kr, MLA_NOPE, axis=1)
    w_in0 = _bf(jnp.concatenate([w_in[:, :i_kr], w_kr, w_in[:, i_p:]], axis=1))
    w_uq = _bf(_mla_head_cols(mla_w_uq[0], MLA_HEADS, MLA_NOPE, MLA_ROPE))
    w_uk = _bf(_mla_head_cols(mla_w_uk[0], MLA_HEADS, MLA_NOPE, 0))
    w_uvt = _bf(mla_w_uv[0].T)
    w_out0 = mix0_w_out[0]
    w_oa = _bf(w_out0[:MLA_HEADS * MLA_V])
    w_op = _bf(w_out0[MLA_HEADS * MLA_V:])

    m0 = mods_of(0)
    q, k, vt, p = _in0(x, ctx, m0, w_in0, mla_g_q[0][None], w_uq, mla_g_kv[0][None], w_uk, w_uvt,
                       _mla_tables())
    mla = dict(n_heads=MLA_HEADS_PER_STEP, shared_kv=False, dv=MLA_V)
    o = _attention(q, k, vt, context=False, **mla)
    o_ctx = _attention(q, k, vt, context=True, **mla)
    xa = _out0(x, ctx, m0, o, o_ctx, p, _bf(pool_w[0]), pool_scale[0][None], w_oa, w_op)
    xa = _ffn(xa, m0, _bf(ffn_w_up[0]), ffn_conv_w[0], ffn_conv_b[0][None], _bf(ffn_w_down[0]),
              g_final[None], final_norm=False)

    perm = _deinterleave(GQA_HEAD_DIM)
    n_qk = GQA_HEADS + GQA_KV_HEADS
    w_in = gqa_w_in[0]
    w_qk = w_in[:, :n_qk * LANES].reshape(D_MODEL, n_qk, LANES)[:, :, perm].reshape(D_MODEL, n_qk * LANES)
    w_vt = _bf(w_in[:, n_qk * LANES:].T)
    m1 = mods_of(1)
    q, k, vt = _in1(xa, m1, _bf(w_qk), w_vt, gqa_g_q[0][perm][None], gqa_g_k[0][perm][None], _gqa_tables())
    o = _attention(q, k, vt, n_heads=GQA_GROUP, shared_kv=True, dv=GQA_HEAD_DIM, context=False)
    xa = _out1(xa, m1, o, _bf(gqa_w_out[0]))
    return _ffn(xa, m1, _bf(ffn_w_up[1]), ffn_conv_w[1], ffn_conv_b[1][None], _bf(ffn_w_down[1]),
                g_final[None], final_norm=True)
```

```python
import functools

import jax
import jax.numpy as jnp
import numpy as np
from jax import lax
from jax.experimental import pallas as pl
from jax.experimental.pallas import tpu as pltpu

D_MODEL = 1024
SEQ = 8192
GRID_W = 64
CTX_LEN = 256
T_ALL = SEQ + CTX_LEN
ROPE_THETA = 10000.0
EPS = 1e-6
N_MOD = 6
MLA_HEADS = 8
MLA_NOPE = 64
MLA_ROPE = 32
MLA_V = 64
MLA_Q_RANK = 384
MLA_KV_RANK = 256
POOL_WINDOWS = (2, 4, 8, 16)
POOL_WIDTH = 512
GQA_HEADS = 8
GQA_KV_HEADS = 2
GQA_HEAD_DIM = 128
GQA_GROUP = GQA_HEADS // GQA_KV_HEADS
D_FF = 2816
CONV_W = 3

LANES = 128
SUBLANES = 8
ROW_TILE = 256
N_LAT_TILES = SEQ // ROW_TILE
N_TILES = T_ALL // ROW_TILE
HALO_BLOCKS_PER_TILE = ROW_TILE // SUBLANES
KV_CHUNK = 768
PAIRS_PER_TRIP = 2
TILES_PER_STEP = 4
FF_CHUNK = 256
PAIR = 2
VMEM_LIMIT = 56 * 1024 * 1024

MIX0_IN_PAD = MLA_Q_RANK + MLA_KV_RANK + LANES + POOL_WIDTH
HEADS_W = MLA_HEADS * LANES
LOG2_E = 1.4426950408889634
ONES_ROWS = 16
MLA_VROWS = MLA_V + ONES_ROWS
GQA_VROWS = GQA_HEAD_DIM + ONES_ROWS
MLA_HEADS_PER_STEP = 4


def _params(n_axes):
    return pltpu.CompilerParams(dimension_semantics=("arbitrary",) * n_axes, vmem_limit_bytes=VMEM_LIMIT)


def _rms(x):
    return x * lax.rsqrt(jnp.mean(x * x, axis=-1, keepdims=True) + EPS)


def _dot(a, b):
    return jnp.dot(a, b, preferred_element_type=jnp.float32)


def _dot_nt(a, b):
    return lax.dot_general(a, b, (((1,), (1,)), ((), ())), preferred_element_type=jnp.float32)


def _bf(x):
    return x.astype(jnp.bfloat16)


def _mod_kernel(c_ref, w_ref, b_ref, o_ref):
    s = jax.nn.silu(c_ref[...])
    o_ref[0] = jnp.dot(s, w_ref[0], preferred_element_type=jnp.float32,
                       precision=lax.Precision.HIGHEST) + b_ref[0]


def _modulation(cc, w_mod, b_mod):
    depth = w_mod.shape[0]
    return pl.pallas_call(
        _mod_kernel,
        out_shape=jax.ShapeDtypeStruct((depth, SUBLANES, N_MOD * D_MODEL), jnp.float32),
        grid=(depth, N_MOD),
        in_specs=[
            pl.BlockSpec((SUBLANES, D_MODEL), lambda i, j: (0, 0)),
            pl.BlockSpec((1, D_MODEL, D_MODEL), lambda i, j: (i, 0, j)),
            pl.BlockSpec((1, 1, D_MODEL), lambda i, j: (i, 0, j)),
        ],
        out_specs=pl.BlockSpec((1, SUBLANES, D_MODEL), lambda i, j: (i, 0, j)),
        compiler_params=_params(2),
        name="modulation",
    )(cc, w_mod, b_mod.reshape(depth, 1, N_MOD * D_MODEL))


def _row_spec(width):
    return pl.BlockSpec((PAIR, ROW_TILE, width), lambda b, t: (b, t, 0))


def _mod_spec():
    return pl.BlockSpec((PAIR, 1, N_MOD, D_MODEL), lambda b, t: (b, t // N_LAT_TILES, 0, 0))


def _const_spec(shape):
    nd = len(shape)
    return pl.BlockSpec(shape, lambda b, t: (0,) * nd)


def _table_spec():
    return pl.BlockSpec((ROW_TILE, LANES), lambda b, t: (t, 0))


def _prev_halo_spec(width):
    return pl.BlockSpec((PAIR, SUBLANES, width),
                        lambda b, t: (b, jnp.maximum(t * HALO_BLOCKS_PER_TILE - 1, 0), 0))


def _next_halo_spec(width, total_rows=T_ALL):
    last = total_rows // SUBLANES - 1
    return pl.BlockSpec((PAIR, SUBLANES, width),
                        lambda b, t: (b, jnp.minimum((t + 1) * HALO_BLOCKS_PER_TILE, last), 0))


def _halo_valid(t):
    prev_ok = jnp.logical_and(t != 0, t != N_LAT_TILES)
    next_ok = jnp.logical_and(t != N_LAT_TILES - 1, t != N_LAT_TILES)
    return prev_ok, next_ok


def _with_halo(prev_ref, main_ref, next_ref, s):
    return jnp.concatenate([prev_ref[s], main_ref[s], next_ref[s]], axis=0)


def _halo_row_mask(prev_ok, next_ok):
    r = lax.broadcasted_iota(jnp.int32, (ROW_TILE + 2 * SUBLANES, 1), 0)
    return jnp.logical_and(jnp.logical_or(r >= SUBLANES, prev_ok),
                           jnp.logical_or(r < ROW_TILE + SUBLANES, next_ok))


def _shift_rows(x, k):
    n = x.shape[0]
    return pltpu.roll(x, k % n, axis=0)[SUBLANES:SUBLANES + ROW_TILE]


def _rope_mla(x, c, s_up, s_dn):
    return x * c + pltpu.roll(x, MLA_ROPE // 2, axis=1) * s_up + pltpu.roll(x, LANES - MLA_ROPE // 2, axis=1) * s_dn


def _two_stream_specs(width=D_MODEL):
    return [pl.BlockSpec((PAIR, ROW_TILE, width), lambda b, t: (b, jnp.minimum(t, N_LAT_TILES - 1), 0)),
            pl.BlockSpec((PAIR, CTX_LEN, width), lambda b, t: (b, 0, 0))]


def _two_stream_rows(x_ref, ctx_ref, s):
    return jnp.where(pl.program_id(1) == N_LAT_TILES, ctx_ref[s], x_ref[s])


def _in0_kernel(x_ref, ctx_ref, mod_ref, w_in_ref, gq_ref, w_uq_ref, gkv_ref, w_uk_ref, w_uvt_ref,
                c_ref, su_ref, sd_ref, q_ref, k_ref, v_ref, p_ref):
    i_kv = MLA_Q_RANK
    i_kr = i_kv + MLA_KV_RANK
    i_p = i_kr + LANES
    c, su, sd = c_ref[...], su_ref[...], sd_ref[...]
    scale = (MLA_NOPE + MLA_ROPE) ** -0.5 * LOG2_E
    pair = range(PAIR)

    def modulated(s):
        m = mod_ref[s, 0]
        return _bf(_rms(_two_stream_rows(x_ref, ctx_ref, s)) * (1.0 + m[1:2]) + m[0:1])

    ones = jnp.ones((ONES_ROWS, ROW_TILE), jnp.bfloat16)
    for s in pair:
        a = _dot(modulated(s), w_in_ref[...])
        q = _dot(_bf(_rms(a[:, :i_kv]) * gq_ref[...]), w_uq_ref[...])
        ckv = _bf(_rms(a[:, i_kv:i_kr]) * gkv_ref[...])
        kn = _dot(ckv, w_uk_ref[...])
        vt = _dot_nt(w_uvt_ref[...], ckv)
        kr = _rope_mla(a[:, i_kr:i_p], c, su, sd)
        for hd in range(MLA_HEADS):
            sl = slice(hd * LANES, (hd + 1) * LANES)
            q_ref[s, :, sl] = _bf(_rope_mla(q[:, sl], c, su, sd) * scale)
            k_ref[s, :, sl] = _bf(kn[:, sl] + kr)
            v_ref[s, hd * MLA_VROWS:hd * MLA_VROWS + MLA_V, :] = _bf(vt[hd * MLA_V:(hd + 1) * MLA_V])
            v_ref[s, hd * MLA_VROWS + MLA_V:(hd + 1) * MLA_VROWS, :] = ones
        p_ref[s] = a[:, i_p:]


def _vt_spec(rows):
    return pl.BlockSpec((PAIR, rows, ROW_TILE), lambda b, t: (b, 0, t))


def _in0(x, ctx, mods, w_in, g_q, w_uq, g_kv, w_uk, w_uvt, tabs):
    B = x.shape[0]
    outs = (
        jax.ShapeDtypeStruct((B, T_ALL, HEADS_W), jnp.bfloat16),
        jax.ShapeDtypeStruct((B, T_ALL, HEADS_W), jnp.bfloat16),
        jax.ShapeDtypeStruct((B, MLA_HEADS * MLA_VROWS, T_ALL), jnp.bfloat16),
        jax.ShapeDtypeStruct((B, T_ALL, POOL_WIDTH), jnp.float32),
    )
    return pl.pallas_call(
        _in0_kernel,
        out_shape=outs,
        grid=(B // PAIR, N_TILES),
        in_specs=_two_stream_specs() + [
            _mod_spec(),
            _const_spec(w_in.shape), _const_spec(g_q.shape), _const_spec(w_uq.shape),
            _const_spec(g_kv.shape), _const_spec(w_uk.shape), _const_spec(w_uvt.shape),
            _table_spec(), _table_spec(), _table_spec(),
        ],
        out_specs=(_row_spec(HEADS_W), _row_spec(HEADS_W), _vt_spec(MLA_HEADS * MLA_VROWS),
                   _row_spec(POOL_WIDTH)),
        compiler_params=_params(2),
        name="mla_pool_in_proj",
    )(x, ctx, mods, w_in, g_q, w_uq, g_kv, w_uk, w_uvt, *tabs)


def _in1_kernel(x_ref, mod_ref, w_in_ref, w_vt_ref, gq_ref, gk_ref, c_ref, s_ref, q_ref, k_ref, v_ref):
    c, sn = c_ref[...], s_ref[...]
    scale = GQA_HEAD_DIM ** -0.5 * LOG2_E
    pair = range(PAIR)

    def modulated(s):
        m = mod_ref[s, 0]
        return _bf(_rms(x_ref[s]) * (1.0 + m[1:2]) + m[0:1])

    ones = jnp.ones((ONES_ROWS, ROW_TILE), jnp.bfloat16)

    for s in pair:
        h = modulated(s)
        a = _dot(h, w_in_ref[...])
        vt = _dot_nt(w_vt_ref[...], h)

        def head(j, gain, a=a):
            xn = _rms(a[:, j * LANES:(j + 1) * LANES]) * gain
            return xn * c + pltpu.roll(xn, LANES // 2, axis=1) * sn

        for j in range(GQA_HEADS):
            q_ref[s, :, j * LANES:(j + 1) * LANES] = _bf(head(j, gq_ref[...]) * scale)
        for j in range(GQA_KV_HEADS):
            k_ref[s, :, j * LANES:(j + 1) * LANES] = _bf(head(GQA_HEADS + j, gk_ref[...]))
            v_ref[s, j * GQA_VROWS:j * GQA_VROWS + GQA_HEAD_DIM, :] = _bf(vt[j * GQA_HEAD_DIM:(j + 1) * GQA_HEAD_DIM])
            v_ref[s, j * GQA_VROWS + GQA_HEAD_DIM:(j + 1) * GQA_VROWS, :] = ones


def _in1(xa, mods, w_in, w_vt, g_q, g_k, tabs):
    B = xa.shape[0]
    kv_w = GQA_KV_HEADS * LANES
    outs = (
        jax.ShapeDtypeStruct((B, T_ALL, HEADS_W), jnp.bfloat16),
        jax.ShapeDtypeStruct((B, T_ALL, kv_w), jnp.bfloat16),
        jax.ShapeDtypeStruct((B, GQA_KV_HEADS * GQA_VROWS, T_ALL), jnp.bfloat16),
    )
    return pl.pallas_call(
        _in1_kernel,
        out_shape=outs,
        grid=(B // PAIR, N_TILES),
        in_specs=[
            _row_spec(D_MODEL), _mod_spec(), _const_spec(w_in.shape), _const_spec(w_vt.shape),
            _const_spec(g_q.shape), _const_spec(g_k.shape), _table_spec(), _table_spec(),
        ],
        out_specs=(_row_spec(HEADS_W), _row_spec(kv_w), _vt_spec(GQA_KV_HEADS * GQA_VROWS)),
        compiler_params=_params(2),
        name="gqa_in_proj",
    )(xa, mods, w_in, w_vt, g_q, g_k, *tabs)


def _score(q, k):
    return _dot_nt(k, q)


def _col_max(s):
    return jnp.max(s, axis=0, keepdims=True)


def _accumulate(s, cm, vt, state):
    m, acc = state
    m_new = jnp.maximum(m, cm)
    p = _bf(jnp.exp2(s - m_new))
    return m_new, jnp.exp2(m - m_new) * acc + _dot(vt, p)


def _attn_views(k_ref, vt_ref, *, shared_kv, vrows):
    def k_of(hd, rows):
        j = 0 if shared_kv else hd
        return k_ref[0, rows, j * LANES:(j + 1) * LANES]

    def vt_of(hd, rows):
        j = 0 if shared_kv else hd
        return vt_ref[0, j * vrows:(j + 1) * vrows, rows]

    return k_of, vt_of


def _attn_init(n_heads, vrows, tq):
    return tuple((jnp.full((1, tq), -jnp.inf, jnp.float32), jnp.zeros((vrows, tq), jnp.float32))
                 for _ in range(n_heads))


def _attn_output(states, dv):
    ot = jnp.concatenate([acc[:dv] / acc[dv:dv + 1] for _, acc in states], axis=0)
    return _bf(ot.T)


def _attn_kernel(q_ref, k_ref, vt_ref, o_ref, *s_refs, n_heads, shared_kv, dv):
    vrows = dv + ONES_ROWS
    heads = range(n_heads)
    n_chunks = T_ALL // KV_CHUNK
    assert n_chunks % 2 == 1 and (n_chunks - 3) % (2 * PAIRS_PER_TRIP) == 0
    k_of, vt_of = _attn_views(k_ref, vt_ref, shared_kv=shared_kv, vrows=vrows)

    def chunk(c):
        return pl.ds(pl.multiple_of(c * KV_CHUNK, KV_CHUNK), KV_CHUNK)

    def qs_of(tile):
        return [q_ref[0, tile * ROW_TILE:(tile + 1) * ROW_TILE, hd * LANES:(hd + 1) * LANES] for hd in heads]

    def park(qs, c, hd, buf):
        s = _score(qs[hd], k_of(hd, chunk(c)))
        buf[hd][...] = s
        return _col_max(s)

    def consume(c, hd, buf, cm, state):
        return _accumulate(buf[hd][...], cm, vt_of(hd, chunk(c)), state)

    def half_step(qs, c_next, c, cur, nxt, cms, states):
        new_cms, new_states = [], []
        for hd in heads:
            new_cms.append(park(qs, c_next, hd, nxt))
            new_states.append(consume(c, hd, cur, cms[hd], states[hd]))
        return tuple(new_cms), tuple(new_states)

    even, odd = s_refs[:n_heads], s_refs[n_heads:]
    cms = tuple(park(qs_of(0), 0, hd, even) for hd in heads)
    for tile in range(TILES_PER_STEP):
        qs = qs_of(tile)

        def pair(c, carry, qs=qs, even=even, odd=odd):
            carry = half_step(qs, c + 1, c, odd, even, *carry)
            return half_step(qs, c + 2, c + 1, even, odd, *carry)

        def body(i, carry, pair=pair):
            for k in range(PAIRS_PER_TRIP):
                carry = pair(1 + 2 * (PAIRS_PER_TRIP * i + k), carry)
            return carry

        carry = half_step(qs, 1, 0, even, odd, cms, _attn_init(n_heads, vrows, ROW_TILE))
        carry = lax.fori_loop(0, (n_chunks - 3) // (2 * PAIRS_PER_TRIP), body, carry)
        cms, states = half_step(qs, n_chunks - 1, n_chunks - 2, odd, even, *carry)
        if tile + 1 < TILES_PER_STEP:
            cms, states = half_step(qs_of(tile + 1), 0, n_chunks - 1, even, odd, cms, states)
            even, odd = odd, even
        else:
            states = [consume(n_chunks - 1, hd, even, cms[hd], states[hd]) for hd in heads]
        o_ref[0, tile * ROW_TILE:(tile + 1) * ROW_TILE, :] = _attn_output(states, dv)


def _ctx_attn_kernel(q_ref, k_ref, vt_ref, o_ref, *, n_heads, shared_kv, dv):
    vrows = dv + ONES_ROWS
    k_of, vt_of = _attn_views(k_ref, vt_ref, shared_kv=shared_kv, vrows=vrows)
    rows = slice(None)
    states = []
    for hd, state in enumerate(_attn_init(n_heads, vrows, CTX_LEN)):
        s = _score(q_ref[0, :, hd * LANES:(hd + 1) * LANES], k_of(hd, rows))
        states.append(_accumulate(s, _col_max(s), vt_of(hd, rows), state))
    o_ref[0] = _attn_output(states, dv)


def _attention(q, k, vt, *, n_heads, shared_kv, dv, context):
    B = q.shape[0]
    groups = q.shape[2] // (n_heads * LANES)
    kv_heads = 1 if shared_kv else n_heads
    vrows = dv + ONES_ROWS
    params = dict(n_heads=n_heads, shared_kv=shared_kv, dv=dv)
    if context:
        last = T_ALL // CTX_LEN - 1
        return pl.pallas_call(
            functools.partial(_ctx_attn_kernel, **params),
            out_shape=jax.ShapeDtypeStruct((B, CTX_LEN, groups * n_heads * dv), jnp.bfloat16),
            grid=(B, groups),
            in_specs=[pl.BlockSpec((1, CTX_LEN, n_heads * LANES), lambda b, g: (b, last, g)),
                      pl.BlockSpec((1, CTX_LEN, kv_heads * LANES), lambda b, g: (b, last, g)),
                      pl.BlockSpec((1, kv_heads * vrows, CTX_LEN), lambda b, g: (b, g, last))],
            out_specs=pl.BlockSpec((1, CTX_LEN, n_heads * dv), lambda b, g: (b, 0, g)),
            compiler_params=_params(2),
            name="context_attention",
        )(q, k, vt)
    rows = TILES_PER_STEP * ROW_TILE
    return pl.pallas_call(
        functools.partial(_attn_kernel, **params),
        out_shape=jax.ShapeDtypeStruct((B, SEQ, groups * n_heads * dv), jnp.bfloat16),
        grid=(B, groups, SEQ // rows),
        in_specs=[pl.BlockSpec((1, rows, n_heads * LANES), lambda b, g, i: (b, i, g)),
                  pl.BlockSpec((1, T_ALL, kv_heads * LANES), lambda b, g, i: (b, 0, g)),
                  pl.BlockSpec((1, kv_heads * vrows, T_ALL), lambda b, g, i: (b, g, 0))],
        out_specs=pl.BlockSpec((1, rows, n_heads * dv), lambda b, g, i: (b, i, g)),
        scratch_shapes=[pltpu.VMEM((KV_CHUNK, ROW_TILE), jnp.float32)] * (2 * n_heads),
        compiler_params=_params(3),
        name="attention",
    )(q, k, vt)


def _out0_kernel(x_ref, ctx_ref, mod_ref, o_ref, oc_ref, pp_ref, p_ref, pn_ref, w_pool_ref, s_pool_ref,
                 w_oa_ref, w_op_ref, y_ref):
    t = pl.program_id(1)
    row_ok = _halo_row_mask(*_halo_valid(t))
    t0 = jnp.where(t == N_LAT_TILES, 0, t * ROW_TILE)
    t_len = jnp.where(t == N_LAT_TILES, CTX_LEN, SEQ)
    pos = t0 + lax.broadcasted_iota(jnp.int32, (ROW_TILE, 1), 0)
    pair = range(PAIR)
    groups = range(len(POOL_WINDOWS))

    def pool_inputs(s):
        pe = jnp.where(row_ok, _with_halo(pp_ref, p_ref, pn_ref, s), 0.0)
        out = []
        for g, w in enumerate(POOL_WINDOWS):
            run = pe[:, g * LANES:(g + 1) * LANES]
            span = 1
            while span < w:
                run = run + pltpu.roll(run, run.shape[0] - span, axis=0)
                span *= 2
            win = _shift_rows(run, w // 2)
            lo = jnp.clip(pos - w // 2, 0, t_len)
            hi = jnp.clip(pos - w // 2 + w, 0, t_len)
            out.append(_bf(win / (hi - lo).astype(jnp.float32) - p_ref[s, :, g * LANES:(g + 1) * LANES]))
        return out

    d = [pool_inputs(s) for s in pair]
    pooled = [[_dot(d[s][g], w_pool_ref[g]) for s in pair] for g in groups]
    pooled = [_bf(jnp.concatenate([pooled[g][s] for g in groups], axis=-1) * s_pool_ref[...]) for s in pair]
    y = [_dot(_two_stream_rows(o_ref, oc_ref, s), w_oa_ref[...]) + _dot(pooled[s], w_op_ref[...]) for s in pair]
    for s in pair:
        y_ref[s] = _two_stream_rows(x_ref, ctx_ref, s) + mod_ref[s, 0][2:3] * y[s]


def _out0(x, ctx, mods, o, o_ctx, p, w_pool, s_pool, w_oa, w_op):
    B = x.shape[0]
    return pl.pallas_call(
        _out0_kernel,
        out_shape=jax.ShapeDtypeStruct((B, T_ALL, D_MODEL), jnp.float32),
        grid=(B // PAIR, N_TILES),
        in_specs=_two_stream_specs() + [_mod_spec()] + _two_stream_specs(o.shape[2]) + [
            _prev_halo_spec(POOL_WIDTH), _row_spec(POOL_WIDTH), _next_halo_spec(POOL_WIDTH),
            _const_spec(w_pool.shape), _const_spec(s_pool.shape),
            _const_spec(w_oa.shape), _const_spec(w_op.shape),
        ],
        out_specs=_row_spec(D_MODEL),
        compiler_params=_params(2),
        name="mla_pool_out_proj",
    )(x, ctx, mods, o, o_ctx, p, p, p, w_pool, s_pool, w_oa, w_op)


def _out1_kernel(x_ref, mod_ref, o_ref, w_ref, y_ref):
    y = [_dot(o_ref[s], w_ref[...]) for s in range(PAIR)]
    for s in range(PAIR):
        y_ref[s] = x_ref[s] + mod_ref[s, 0][2:3] * y[s]


def _out1(xa, mods, o, w_out):
    B = xa.shape[0]
    return pl.pallas_call(
        _out1_kernel,
        out_shape=jax.ShapeDtypeStruct((B, SEQ, D_MODEL), jnp.float32),
        grid=(B // PAIR, N_LAT_TILES),
        in_specs=[_row_spec(D_MODEL), _mod_spec(), _row_spec(HEADS_W), _const_spec(w_out.shape)],
        out_specs=_row_spec(D_MODEL),
        compiler_params=_params(2),
        name="gqa_out_proj",
    )(xa, mods, o, w_out)


def _ffn_kernel(xp_ref, x_ref, xn_ref, mod_ref, w_up_ref, cw_ref, cb_ref, w_dn_ref, gf_ref, y_ref,
                *, final_norm):
    row_ok = _halo_row_mask(*_halo_valid(pl.program_id(1)))
    pair = range(PAIR)

    def modulated(s):
        m = mod_ref[s, 0]
        he = _rms(_with_halo(xp_ref, x_ref, xn_ref, s)) * (1.0 + m[4:5]) + m[3:4]
        return _bf(jnp.where(row_ok, he, 0.0))

    he = [modulated(s) for s in pair]

    def up(s, cs):
        g = _dot(he[s], w_up_ref[:, cs])
        u = _dot(he[s][SUBLANES:SUBLANES + ROW_TILE], w_up_ref[:, D_FF + cs.start:D_FF + cs.stop])
        return g, u

    def down(cs, g, u):
        acc = cb_ref[:, cs] + _shift_rows(g, 1) * cw_ref[0:1, cs]
        acc = acc + g[SUBLANES:SUBLANES + ROW_TILE] * cw_ref[1:2, cs]
        acc = acc + _shift_rows(g, -1) * cw_ref[2:3, cs]
        return _dot(_bf(jax.nn.silu(acc) * u), w_dn_ref[cs, :])

    chunks = [slice(c0, c0 + FF_CHUNK) for c0 in range(0, D_FF, FF_CHUNK)]
    y = [jnp.zeros((ROW_TILE, D_MODEL), jnp.float32) for _ in pair]
    pending = [up(s, chunks[0]) for s in pair]
    for cs, cs_next in zip(chunks, chunks[1:] + [None]):
        ahead = [up(s, cs_next) for s in pair] if cs_next is not None else None
        y = [y[s] + down(cs, *pending[s]) for s in pair]
        pending = ahead
    for s in pair:
        out = x_ref[s] + mod_ref[s, 0][5:6] * y[s]
        if final_norm:
            out = _rms(out) * gf_ref[...]
        y_ref[s] = out


def _ffn(xa, mods, w_up, conv_w, conv_b, w_dn, g_final, *, final_norm):
    B = xa.shape[0]
    rows = xa.shape[1]
    n_t = rows // ROW_TILE
    assert final_norm == (rows == SEQ)
    return pl.pallas_call(
        functools.partial(_ffn_kernel, final_norm=final_norm),
        out_shape=jax.ShapeDtypeStruct((B, rows, D_MODEL), jnp.float32),
        grid=(B // PAIR, n_t),
        in_specs=[
            _prev_halo_spec(D_MODEL), _row_spec(D_MODEL), _next_halo_spec(D_MODEL, rows), _mod_spec(),
            _const_spec(w_up.shape), _const_spec(conv_w.shape), _const_spec(conv_b.shape),
            _const_spec(w_dn.shape), _const_spec(g_final.shape),
        ],
        out_specs=_row_spec(D_MODEL),
        compiler_params=_params(2),
        name="conv_ffn",
    )(xa, xa, xa, mods, w_up, conv_w, conv_b, w_dn, g_final)


def _rope_angles(rope_dim):
    rows = SEQ // GRID_W
    row = jnp.repeat(jnp.arange(rows, dtype=jnp.float32), GRID_W)
    col = jnp.tile(jnp.arange(GRID_W, dtype=jnp.float32), rows)
    n_freq = rope_dim // 4
    freq = ROPE_THETA ** (-jnp.arange(n_freq, dtype=jnp.float32) / n_freq)
    ang = jnp.concatenate([row[:, None] * freq, col[:, None] * freq], axis=-1)
    return jnp.concatenate([ang, jnp.zeros((CTX_LEN, rope_dim // 2), jnp.float32)], axis=0)


def _mla_tables():
    ang = _rope_angles(MLA_ROPE)
    cos, sin = jnp.cos(ang), jnp.sin(ang)
    half = MLA_ROPE // 2
    ones = jnp.ones((T_ALL, MLA_NOPE), jnp.float32)
    zn = jnp.zeros((T_ALL, MLA_NOPE), jnp.float32)
    zh = jnp.zeros((T_ALL, half), jnp.float32)
    zt = jnp.zeros((T_ALL, LANES - MLA_NOPE - MLA_ROPE), jnp.float32)
    c = jnp.concatenate([ones, cos, cos, zt], axis=-1)
    s_up = jnp.concatenate([zn, zh, sin, zt], axis=-1)
    s_dn = jnp.concatenate([zn, -sin, zh, zt], axis=-1)
    return c, s_up, s_dn


def _gqa_tables():
    ang = _rope_angles(GQA_HEAD_DIM)
    cos, sin = jnp.cos(ang), jnp.sin(ang)
    return jnp.concatenate([cos, cos], axis=-1), jnp.concatenate([-sin, sin], axis=-1)


def _mla_head_cols(w, n_heads, nope, rope):
    K = w.shape[0]
    w = w.reshape(K, n_heads, nope + rope)
    parts = [w[..., :nope]]
    if rope:
        parts += [w[..., nope::2], w[..., nope + 1::2]]
    parts.append(jnp.zeros((K, n_heads, LANES - nope - rope), w.dtype))
    return jnp.concatenate(parts, axis=-1).reshape(K, n_heads * LANES)


def _deinterleave(n):
    return np.concatenate([np.arange(0, n, 2), np.arange(1, n, 2)])


def kernel(x, c, ctx, c_ctx, w_mod, b_mod, mix0_w_in, mla_g_q, mla_w_uq, mla_g_kv, mla_w_uk, mla_w_uv,
           pool_w, pool_scale, mix0_w_out, gqa_w_in, gqa_g_q, gqa_g_k, gqa_w_out,
           ffn_w_up, ffn_conv_w, ffn_conv_b, ffn_w_down, g_final):
    B = x.shape[0]
    assert x.shape == (B, SEQ, D_MODEL) and ctx.shape == (B, CTX_LEN, D_MODEL) and B < SUBLANES and B % PAIR == 0
    assert w_mod.shape[0] == 2

    cc = jnp.zeros((SUBLANES, D_MODEL), jnp.float32).at[:B].set(c).at[B].set(c_ctx)
    mod_all = _modulation(cc, w_mod, b_mod).reshape(2, SUBLANES, N_MOD, D_MODEL)

    def mods_of(i):
        lat = mod_all[i, :B]
        con = jnp.broadcast_to(mod_all[i, B], lat.shape)
        return jnp.stack([lat, con], axis=1)

    w_in = mix0_w_in[0]
    i_kv = MLA_Q_RANK
    i_kr = i_kv + MLA_KV_RANK
    i_p = i_kr + MLA_ROPE
    w_kr = _mla_head_cols(w_in[:, i_kr:i_p], 1, 0, MLA_ROPE)
    w_kr = jnp.roll(w_kr, MLA_NOPE, axis=1)
    w_in0 = _bf(jnp.concatenate([w_in[:, :i_kr], w_kr, w_in[:, i_p:]], axis=1))
    w_uq = _bf(_mla_head_cols(mla_w_uq[0], MLA_HEADS, MLA_NOPE, MLA_ROPE))
    w_uk = _bf(_mla_head_cols(mla_w_uk[0], MLA_HEADS, MLA_NOPE, 0))
    w_uvt = _bf(mla_w_uv[0].T)
    w_out0 = mix0_w_out[0]
    w_oa = _bf(w_out0[:MLA_HEADS * MLA_V])
    w_op = _bf(w_out0[MLA_HEADS * MLA_V:])

    m0 = mods_of(0)
    q, k, vt, p = _in0(x, ctx, m0, w_in0, mla_g_q[0][None], w_uq, mla_g_kv[0][None], w_uk, w_uvt,
                       _mla_tables())
    mla = dict(n_heads=MLA_HEADS_PER_STEP, shared_kv=False, dv=MLA_V)
    o = _attention(q, k, vt, context=False, **mla)
    o_ctx = _attention(q, k, vt, context=True, **mla)
    xa = _out0(x, ctx, m0, o, o_ctx, p, _bf(pool_w[0]), pool_scale[0][None], w_oa, w_op)
    xa = _ffn(xa, m0, _bf(ffn_w_up[0]), ffn_conv_w[0], ffn_conv_b[0][None], _bf(ffn_w_down[0]),
              g_final[None], final_norm=False)

    perm = _deinterleave(GQA_HEAD_DIM)
    n_qk = GQA_HEADS + GQA_KV_HEADS
    w_in = gqa_w_in[0]
    w_qk = w_in[:, :n_qk * LANES].reshape(D_MODEL, n_qk, LANES)[:, :, perm].reshape(D_MODEL, n_qk * LANES)
    w_vt = _bf(w_in[:, n_qk * LANES:].T)
    m1 = mods_of(1)
    q, k, vt = _in1(xa, m1, _bf(w_qk), w_vt, gqa_g_q[0][perm][None], gqa_g_k[0][perm][None], _gqa_tables())
    o = _attention(q, k, vt, n_heads=GQA_GROUP, shared_kv=True, dv=GQA_HEAD_DIM, context=False)
    xa = _out1(xa, m1, o, _bf(gqa_w_out[0]))
    return _ffn(xa, m1, _bf(ffn_w_up[1]), ffn_conv_w[1], ffn_conv_b[1][None], _bf(ffn_w_down[1]),
                g_final[None], final_norm=True)
```

```python
import functools

import jax
import jax.numpy as jnp
import numpy as np
from jax import lax
from jax.experimental import pallas as pl
from jax.experimental.pallas import tpu as pltpu

D_MODEL = 1024
SEQ = 8192
GRID_W = 64
CTX_LEN = 256
T_ALL = SEQ + CTX_LEN
ROPE_THETA = 10000.0
EPS = 1e-6
N_MOD = 6
MLA_HEADS = 8
MLA_NOPE = 64
MLA_ROPE = 32
MLA_V = 64
MLA_Q_RANK = 384
MLA_KV_RANK = 256
POOL_WINDOWS = (2, 4, 8, 16)
POOL_WIDTH = 512
GQA_HEADS = 8
GQA_KV_HEADS = 2
GQA_HEAD_DIM = 128
GQA_GROUP = GQA_HEADS // GQA_KV_HEADS
D_FF = 2816
CONV_W = 3

LANES = 128
SUBLANES = 8
ROW_TILE = 256
N_LAT_TILES = SEQ // ROW_TILE
N_TILES = T_ALL // ROW_TILE
HALO_BLOCKS_PER_TILE = ROW_TILE // SUBLANES
KV_CHUNK = 768
PAIRS_PER_TRIP = 2
TILES_PER_STEP = 4
FF_CHUNK = 256
PAIR = 2
PROJ_PAIR = 4
VMEM_LIMIT = 56 * 1024 * 1024

MIX0_IN_PAD = MLA_Q_RANK + MLA_KV_RANK + LANES + POOL_WIDTH
HEADS_W = MLA_HEADS * LANES
LOG2_E = 1.4426950408889634
ONES_ROWS = 16
MLA_VROWS = MLA_V + ONES_ROWS
GQA_VROWS = GQA_HEAD_DIM + ONES_ROWS
MLA_HEADS_PER_STEP = 4


def _params(n_axes):
    return pltpu.CompilerParams(dimension_semantics=("arbitrary",) * n_axes, vmem_limit_bytes=VMEM_LIMIT)


def _rms(x):
    return x * lax.rsqrt(jnp.mean(x * x, axis=-1, keepdims=True) + EPS)


def _dot(a, b):
    return jnp.dot(a, b, preferred_element_type=jnp.float32)


def _dot_nt(a, b):
    return lax.dot_general(a, b, (((1,), (1,)), ((), ())), preferred_element_type=jnp.float32)


def _bf(x):
    return x.astype(jnp.bfloat16)


def _mod_kernel(c_ref, w_ref, b_ref, o_ref):
    s = jax.nn.silu(c_ref[...])
    o_ref[0] = jnp.dot(s, w_ref[0], preferred_element_type=jnp.float32,
                       precision=lax.Precision.HIGHEST) + b_ref[0]


def _modulation(cc, w_mod, b_mod):
    depth = w_mod.shape[0]
    return pl.pallas_call(
        _mod_kernel,
        out_shape=jax.ShapeDtypeStruct((depth, SUBLANES, N_MOD * D_MODEL), jnp.float32),
        grid=(depth, N_MOD),
        in_specs=[
            pl.BlockSpec((SUBLANES, D_MODEL), lambda i, j: (0, 0)),
            pl.BlockSpec((1, D_MODEL, D_MODEL), lambda i, j: (i, 0, j)),
            pl.BlockSpec((1, 1, D_MODEL), lambda i, j: (i, 0, j)),
        ],
        out_specs=pl.BlockSpec((1, SUBLANES, D_MODEL), lambda i, j: (i, 0, j)),
        compiler_params=_params(2),
        name="modulation",
    )(cc, w_mod, b_mod.reshape(depth, 1, N_MOD * D_MODEL))


def _row_spec(width, pair=PAIR):
    return pl.BlockSpec((pair, ROW_TILE, width), lambda b, t: (b, t, 0))


def _mod_spec(pair=PAIR):
    return pl.BlockSpec((pair, 1, N_MOD, D_MODEL), lambda b, t: (b, t // N_LAT_TILES, 0, 0))


def _const_spec(shape):
    nd = len(shape)
    return pl.BlockSpec(shape, lambda b, t: (0,) * nd)


def _table_spec():
    return pl.BlockSpec((ROW_TILE, LANES), lambda b, t: (t, 0))


def _prev_halo_spec(width, pair=PAIR):
    return pl.BlockSpec((pair, SUBLANES, width),
                        lambda b, t: (b, jnp.maximum(t * HALO_BLOCKS_PER_TILE - 1, 0), 0))


def _next_halo_spec(width, total_rows=T_ALL, pair=PAIR):
    last = total_rows // SUBLANES - 1
    return pl.BlockSpec((pair, SUBLANES, width),
                        lambda b, t: (b, jnp.minimum((t + 1) * HALO_BLOCKS_PER_TILE, last), 0))


def _halo_valid(t):
    prev_ok = jnp.logical_and(t != 0, t != N_LAT_TILES)
    next_ok = jnp.logical_and(t != N_LAT_TILES - 1, t != N_LAT_TILES)
    return prev_ok, next_ok


def _with_halo(prev_ref, main_ref, next_ref, s):
    return jnp.concatenate([prev_ref[s], main_ref[s], next_ref[s]], axis=0)


def _halo_row_mask(prev_ok, next_ok):
    r = lax.broadcasted_iota(jnp.int32, (ROW_TILE + 2 * SUBLANES, 1), 0)
    return jnp.logical_and(jnp.logical_or(r >= SUBLANES, prev_ok),
                           jnp.logical_or(r < ROW_TILE + SUBLANES, next_ok))


def _shift_rows(x, k):
    n = x.shape[0]
    return pltpu.roll(x, k % n, axis=0)[SUBLANES:SUBLANES + ROW_TILE]


def _rope_mla(x, c, s_up, s_dn):
    return x * c + pltpu.roll(x, MLA_ROPE // 2, axis=1) * s_up + pltpu.roll(x, LANES - MLA_ROPE // 2, axis=1) * s_dn


def _two_stream_specs(width=D_MODEL, pair=PAIR):
    return [pl.BlockSpec((pair, ROW_TILE, width), lambda b, t: (b, jnp.minimum(t, N_LAT_TILES - 1), 0)),
            pl.BlockSpec((pair, CTX_LEN, width), lambda b, t: (b, 0, 0))]


def _two_stream_rows(x_ref, ctx_ref, s):
    return jnp.where(pl.program_id(1) == N_LAT_TILES, ctx_ref[s], x_ref[s])


def _in0_kernel(x_ref, ctx_ref, mod_ref, w_in_ref, gq_ref, w_uq_ref, gkv_ref, w_uk_ref, w_uvt_ref,
                c_ref, su_ref, sd_ref, q_ref, k_ref, v_ref, p_ref):
    i_kv = MLA_Q_RANK
    i_kr = i_kv + MLA_KV_RANK
    i_p = i_kr + LANES
    c, su, sd = c_ref[...], su_ref[...], sd_ref[...]
    scale = (MLA_NOPE + MLA_ROPE) ** -0.5 * LOG2_E
    pair = range(x_ref.shape[0])

    def modulated(s):
        m = mod_ref[s, 0]
        return _bf(_rms(_two_stream_rows(x_ref, ctx_ref, s)) * (1.0 + m[1:2]) + m[0:1])

    ones = jnp.ones((ONES_ROWS, ROW_TILE), jnp.bfloat16)
    for s in pair:
        a = _dot(modulated(s), w_in_ref[...])
        q = _dot(_bf(_rms(a[:, :i_kv]) * gq_ref[...]), w_uq_ref[...])
        ckv = _bf(_rms(a[:, i_kv:i_kr]) * gkv_ref[...])
        kn = _dot(ckv, w_uk_ref[...])
        vt = _dot_nt(w_uvt_ref[...], ckv)
        kr = _rope_mla(a[:, i_kr:i_p], c, su, sd)
        for hd in range(MLA_HEADS):
            sl = slice(hd * LANES, (hd + 1) * LANES)
            q_ref[s, :, sl] = _bf(_rope_mla(q[:, sl], c, su, sd) * scale)
            k_ref[s, :, sl] = _bf(kn[:, sl] + kr)
            v_ref[s, hd * MLA_VROWS:hd * MLA_VROWS + MLA_V, :] = _bf(vt[hd * MLA_V:(hd + 1) * MLA_V])
            v_ref[s, hd * MLA_VROWS + MLA_V:(hd + 1) * MLA_VROWS, :] = ones
        p_ref[s] = a[:, i_p:]


def _vt_spec(rows, pair=PAIR):
    return pl.BlockSpec((pair, rows, ROW_TILE), lambda b, t: (b, 0, t))


def _in0(x, ctx, mods, w_in, g_q, w_uq, g_kv, w_uk, w_uvt, tabs):
    B = x.shape[0]
    outs = (
        jax.ShapeDtypeStruct((B, T_ALL, HEADS_W), jnp.bfloat16),
        jax.ShapeDtypeStruct((B, T_ALL, HEADS_W), jnp.bfloat16),
        jax.ShapeDtypeStruct((B, MLA_HEADS * MLA_VROWS, T_ALL), jnp.bfloat16),
        jax.ShapeDtypeStruct((B, T_ALL, POOL_WIDTH), jnp.float32),
    )
    return pl.pallas_call(
        _in0_kernel,
        out_shape=outs,
        grid=(B // PROJ_PAIR, N_TILES),
        in_specs=_two_stream_specs(pair=PROJ_PAIR) + [
            _mod_spec(PROJ_PAIR),
            _const_spec(w_in.shape), _const_spec(g_q.shape), _const_spec(w_uq.shape),
            _const_spec(g_kv.shape), _const_spec(w_uk.shape), _const_spec(w_uvt.shape),
            _table_spec(), _table_spec(), _table_spec(),
        ],
        out_specs=(_row_spec(HEADS_W, PROJ_PAIR), _row_spec(HEADS_W, PROJ_PAIR),
                   _vt_spec(MLA_HEADS * MLA_VROWS, PROJ_PAIR), _row_spec(POOL_WIDTH, PROJ_PAIR)),
        compiler_params=_params(2),
        name="mla_pool_in_proj",
    )(x, ctx, mods, w_in, g_q, w_uq, g_kv, w_uk, w_uvt, *tabs)


def _in1_kernel(x_ref, mod_ref, w_in_ref, w_vt_ref, gq_ref, gk_ref, c_ref, s_ref, q_ref, k_ref, v_ref):
    c, sn = c_ref[...], s_ref[...]
    scale = GQA_HEAD_DIM ** -0.5 * LOG2_E
    pair = range(x_ref.shape[0])

    def modulated(s):
        m = mod_ref[s, 0]
        return _bf(_rms(x_ref[s]) * (1.0 + m[1:2]) + m[0:1])

    ones = jnp.ones((ONES_ROWS, ROW_TILE), jnp.bfloat16)

    for s in pair:
        h = modulated(s)
        a = _dot(h, w_in_ref[...])
        vt = _dot_nt(w_vt_ref[...], h)

        def head(j, gain, a=a):
            xn = _rms(a[:, j * LANES:(j + 1) * LANES]) * gain
            return xn * c + pltpu.roll(xn, LANES // 2, axis=1) * sn

        for j in range(GQA_HEADS):
            q_ref[s, :, j * LANES:(j + 1) * LANES] = _bf(head(j, gq_ref[...]) * scale)
        for j in range(GQA_KV_HEADS):
            k_ref[s, :, j * LANES:(j + 1) * LANES] = _bf(head(GQA_HEADS + j, gk_ref[...]))
            v_ref[s, j * GQA_VROWS:j * GQA_VROWS + GQA_HEAD_DIM, :] = _bf(vt[j * GQA_HEAD_DIM:(j + 1) * GQA_HEAD_DIM])
            v_ref[s, j * GQA_VROWS + GQA_HEAD_DIM:(j + 1) * GQA_VROWS, :] = ones


def _in1(xa, mods, w_in, w_vt, g_q, g_k, tabs):
    B = xa.shape[0]
    kv_w = GQA_KV_HEADS * LANES
    outs = (
        jax.ShapeDtypeStruct((B, T_ALL, HEADS_W), jnp.bfloat16),
        jax.ShapeDtypeStruct((B, T_ALL, kv_w), jnp.bfloat16),
        jax.ShapeDtypeStruct((B, GQA_KV_HEADS * GQA_VROWS, T_ALL), jnp.bfloat16),
    )
    return pl.pallas_call(
        _in1_kernel,
        out_shape=outs,
        grid=(B // PAIR, N_TILES),
        in_specs=[
            _row_spec(D_MODEL), _mod_spec(), _const_spec(w_in.shape), _const_spec(w_vt.shape),
            _const_spec(g_q.shape), _const_spec(g_k.shape), _table_spec(), _table_spec(),
        ],
        out_specs=(_row_spec(HEADS_W), _row_spec(kv_w), _vt_spec(GQA_KV_HEADS * GQA_VROWS)),
        compiler_params=_params(2),
        name="gqa_in_proj",
    )(xa, mods, w_in, w_vt, g_q, g_k, *tabs)


def _score(q, k):
    return _dot_nt(k, q)


def _col_max(s):
    return jnp.max(s, axis=0, keepdims=True)


def _accumulate(s, cm, vt, state):
    m, acc = state
    m_new = jnp.maximum(m, cm)
    p = _bf(jnp.exp2(s - m_new))
    return m_new, jnp.exp2(m - m_new) * acc + _dot(vt, p)


def _attn_views(k_ref, vt_ref, *, shared_kv, vrows):
    def k_of(hd, rows):
        j = 0 if shared_kv else hd
        return k_ref[0, rows, j * LANES:(j + 1) * LANES]

    def vt_of(hd, rows):
        j = 0 if shared_kv else hd
        return vt_ref[0, j * vrows:(j + 1) * vrows, rows]

    return k_of, vt_of


def _attn_init(n_heads, vrows, tq):
    return tuple((jnp.full((1, tq), -jnp.inf, jnp.float32), jnp.zeros((vrows, tq), jnp.float32))
                 for _ in range(n_heads))


def _attn_output(states, dv):
    ot = jnp.concatenate([acc[:dv] / acc[dv:dv + 1] for _, acc in states], axis=0)
    return _bf(ot.T)


def _attn_kernel(q_ref, k_ref, vt_ref, o_ref, *s_refs, n_heads, shared_kv, dv):
    vrows = dv + ONES_ROWS
    heads = range(n_heads)
    n_chunks = T_ALL // KV_CHUNK
    assert n_chunks % 2 == 1 and (n_chunks - 3) % (2 * PAIRS_PER_TRIP) == 0
    k_of, vt_of = _attn_views(k_ref, vt_ref, shared_kv=shared_kv, vrows=vrows)

    def chunk(c):
        return pl.ds(pl.multiple_of(c * KV_CHUNK, KV_CHUNK), KV_CHUNK)

    def qs_of(tile):
        return [q_ref[0, tile * ROW_TILE:(tile + 1) * ROW_TILE, hd * LANES:(hd + 1) * LANES] for hd in heads]

    def park(qs, c, hd, buf):
        s = _score(qs[hd], k_of(hd, chunk(c)))
        buf[hd][...] = s
        return _col_max(s)

    def consume(c, hd, buf, cm, state):
        return _accumulate(buf[hd][...], cm, vt_of(hd, chunk(c)), state)

    def half_step(qs, c_next, c, cur, nxt, cms, states):
        new_cms, new_states = [], []
        for hd in heads:
            new_cms.append(park(qs, c_next, hd, nxt))
            new_states.append(consume(c, hd, cur, cms[hd], states[hd]))
        return tuple(new_cms), tuple(new_states)

    even, odd = s_refs[:n_heads], s_refs[n_heads:]
    cms = tuple(park(qs_of(0), 0, hd, even) for hd in heads)
    for tile in range(TILES_PER_STEP):
        qs = qs_of(tile)

        def pair(c, carry, qs=qs, even=even, odd=odd):
            carry = half_step(qs, c + 1, c, odd, even, *carry)
            return half_step(qs, c + 2, c + 1, even, odd, *carry)

        def body(i, carry, pair=pair):
            for k in range(PAIRS_PER_TRIP):
                carry = pair(1 + 2 * (PAIRS_PER_TRIP * i + k), carry)
            return carry

        carry = half_step(qs, 1, 0, even, odd, cms, _attn_init(n_heads, vrows, ROW_TILE))
        carry = lax.fori_loop(0, (n_chunks - 3) // (2 * PAIRS_PER_TRIP), body, carry)
        cms, states = half_step(qs, n_chunks - 1, n_chunks - 2, odd, even, *carry)
        if tile + 1 < TILES_PER_STEP:
            cms, states = half_step(qs_of(tile + 1), 0, n_chunks - 1, even, odd, cms, states)
            even, odd = odd, even
        else:
            states = [consume(n_chunks - 1, hd, even, cms[hd], states[hd]) for hd in heads]
        o_ref[0, tile * ROW_TILE:(tile + 1) * ROW_TILE, :] = _attn_output(states, dv)


def _ctx_attn_kernel(q_ref, k_ref, vt_ref, o_ref, *, n_heads, shared_kv, dv):
    vrows = dv + ONES_ROWS
    k_of, vt_of = _attn_views(k_ref, vt_ref, shared_kv=shared_kv, vrows=vrows)
    rows = slice(None)
    states = []
    for hd, state in enumerate(_attn_init(n_heads, vrows, CTX_LEN)):
        s = _score(q_ref[0, :, hd * LANES:(hd + 1) * LANES], k_of(hd, rows))
        states.append(_accumulate(s, _col_max(s), vt_of(hd, rows), state))
    o_ref[0] = _attn_output(states, dv)


def _attention(q, k, vt, *, n_heads, shared_kv, dv, context):
    B = q.shape[0]
    groups = q.shape[2] // (n_heads * LANES)
    kv_heads = 1 if shared_kv else n_heads
    vrows = dv + ONES_ROWS
    params = dict(n_heads=n_heads, shared_kv=shared_kv, dv=dv)
    if context:
        last = T_ALL // CTX_LEN - 1
        return pl.pallas_call(
            functools.partial(_ctx_attn_kernel, **params),
            out_shape=jax.ShapeDtypeStruct((B, CTX_LEN, groups * n_heads * dv), jnp.bfloat16),
            grid=(B, groups),
            in_specs=[pl.BlockSpec((1, CTX_LEN, n_heads * LANES), lambda b, g: (b, last, g)),
                      pl.BlockSpec((1, CTX_LEN, kv_heads * LANES), lambda b, g: (b, last, g)),
                      pl.BlockSpec((1, kv_heads * vrows, CTX_LEN), lambda b, g: (b, g, last))],
            out_specs=pl.BlockSpec((1, CTX_LEN, n_heads * dv), lambda b, g: (b, 0, g)),
            compiler_params=_params(2),
            name="context_attention",
        )(q, k, vt)
    rows = TILES_PER_STEP * ROW_TILE
    return pl.pallas_call(
        functools.partial(_attn_kernel, **params),
        out_shape=jax.ShapeDtypeStruct((B, SEQ, groups * n_heads * dv), jnp.bfloat16),
        grid=(B, groups, SEQ // rows),
        in_specs=[pl.BlockSpec((1, rows, n_heads * LANES), lambda b, g, i: (b, i, g)),
                  pl.BlockSpec((1, T_ALL, kv_heads * LANES), lambda b, g, i: (b, 0, g)),
                  pl.BlockSpec((1, kv_heads * vrows, T_ALL), lambda b, g, i: (b, g, 0))],
        out_specs=pl.BlockSpec((1, rows, n_heads * dv), lambda b, g, i: (b, i, g)),
        scratch_shapes=[pltpu.VMEM((KV_CHUNK, ROW_TILE), jnp.float32)] * (2 * n_heads),
        compiler_params=_params(3),
        name="attention",
    )(q, k, vt)


def _out0_kernel(x_ref, ctx_ref, mod_ref, o_ref, oc_ref, pp_ref, p_ref, pn_ref, w_pool_ref, s_pool_ref,
                 w_oa_ref, w_op_ref, y_ref):
    t = pl.program_id(1)
    row_ok = _halo_row_mask(*_halo_valid(t))
    t0 = jnp.where(t == N_LAT_TILES, 0, t * ROW_TILE)
    t_len = jnp.where(t == N_LAT_TILES, CTX_LEN, SEQ)
    pos = t0 + lax.broadcasted_iota(jnp.int32, (ROW_TILE, 1), 0)
    pair = range(x_ref.shape[0])
    groups = range(len(POOL_WINDOWS))

    def pool_inputs(s):
        pe = jnp.where(row_ok, _with_halo(pp_ref, p_ref, pn_ref, s), 0.0)
        out = []
        for g, w in enumerate(POOL_WINDOWS):
            run = pe[:, g * LANES:(g + 1) * LANES]
            span = 1
            while span < w:
                run = run + pltpu.roll(run, run.shape[0] - span, axis=0)
                span *= 2
            win = _shift_rows(run, w // 2)
            lo = jnp.clip(pos - w // 2, 0, t_len)
            hi = jnp.clip(pos - w // 2 + w, 0, t_len)
            out.append(_bf(win / (hi - lo).astype(jnp.float32) - p_ref[s, :, g * LANES:(g + 1) * LANES]))
        return out

    d = [pool_inputs(s) for s in pair]
    pooled = [[_dot(d[s][g], w_pool_ref[g]) for s in pair] for g in groups]
    pooled = [_bf(jnp.concatenate([pooled[g][s] for g in groups], axis=-1) * s_pool_ref[...]) for s in pair]
    y = [_dot(_two_stream_rows(o_ref, oc_ref, s), w_oa_ref[...]) + _dot(pooled[s], w_op_ref[...]) for s in pair]
    for s in pair:
        y_ref[s] = _two_stream_rows(x_ref, ctx_ref, s) + mod_ref[s, 0][2:3] * y[s]


def _out0(x, ctx, mods, o, o_ctx, p, w_pool, s_pool, w_oa, w_op):
    B = x.shape[0]
    return pl.pallas_call(
        _out0_kernel,
        out_shape=jax.ShapeDtypeStruct((B, T_ALL, D_MODEL), jnp.float32),
        grid=(B // PROJ_PAIR, N_TILES),
        in_specs=_two_stream_specs(pair=PROJ_PAIR) + [_mod_spec(PROJ_PAIR)] + _two_stream_specs(o.shape[2], PROJ_PAIR) + [
            _prev_halo_spec(POOL_WIDTH, PROJ_PAIR), _row_spec(POOL_WIDTH, PROJ_PAIR),
            _next_halo_spec(POOL_WIDTH, pair=PROJ_PAIR),
            _const_spec(w_pool.shape), _const_spec(s_pool.shape),
            _const_spec(w_oa.shape), _const_spec(w_op.shape),
        ],
        out_specs=_row_spec(D_MODEL, PROJ_PAIR),
        compiler_params=_params(2),
        name="mla_pool_out_proj",
    )(x, ctx, mods, o, o_ctx, p, p, p, w_pool, s_pool, w_oa, w_op)


def _out1_kernel(x_ref, mod_ref, o_ref, w_ref, y_ref):
    pair = range(x_ref.shape[0])
    y = [_dot(o_ref[s], w_ref[...]) for s in pair]
    for s in pair:
        y_ref[s] = x_ref[s] + mod_ref[s, 0][2:3] * y[s]


def _out1(xa, mods, o, w_out):
    B = xa.shape[0]
    return pl.pallas_call(
        _out1_kernel,
        out_shape=jax.ShapeDtypeStruct((B, SEQ, D_MODEL), jnp.float32),
        grid=(B // PROJ_PAIR, N_LAT_TILES),
        in_specs=[_row_spec(D_MODEL, PROJ_PAIR), _mod_spec(PROJ_PAIR), _row_spec(HEADS_W, PROJ_PAIR),
                  _const_spec(w_out.shape)],
        out_specs=_row_spec(D_MODEL, PROJ_PAIR),
        compiler_params=_params(2),
        name="gqa_out_proj",
    )(xa, mods, o, w_out)


def _ffn_kernel(xp_ref, x_ref, xn_ref, mod_ref, w_up_ref, cw_ref, cb_ref, w_dn_ref, gf_ref, y_ref,
                *, final_norm):
    row_ok = _halo_row_mask(*_halo_valid(pl.program_id(1)))
    pair = range(x_ref.shape[0])

    def modulated(s):
        m = mod_ref[s, 0]
        he = _rms(_with_halo(xp_ref, x_ref, xn_ref, s)) * (1.0 + m[4:5]) + m[3:4]
        return _bf(jnp.where(row_ok, he, 0.0))

    he = [modulated(s) for s in pair]

    def up(s, cs):
        g = _dot(he[s], w_up_ref[:, cs])
        u = _dot(he[s][SUBLANES:SUBLANES + ROW_TILE], w_up_ref[:, D_FF + cs.start:D_FF + cs.stop])
        return g, u

    def down(cs, g, u):
        acc = cb_ref[:, cs] + _shift_rows(g, 1) * cw_ref[0:1, cs]
        acc = acc + g[SUBLANES:SUBLANES + ROW_TILE] * cw_ref[1:2, cs]
        acc = acc + _shift_rows(g, -1) * cw_ref[2:3, cs]
        return _dot(_bf(jax.nn.silu(acc) * u), w_dn_ref[cs, :])

    chunks = [slice(c0, c0 + FF_CHUNK) for c0 in range(0, D_FF, FF_CHUNK)]
    y = [jnp.zeros((ROW_TILE, D_MODEL), jnp.float32) for _ in pair]
    pending = [up(s, chunks[0]) for s in pair]
    for cs, cs_next in zip(chunks, chunks[1:] + [None]):
        ahead = [up(s, cs_next) for s in pair] if cs_next is not None else None
        y = [y[s] + down(cs, *pending[s]) for s in pair]
        pending = ahead
    for s in pair:
        out = x_ref[s] + mod_ref[s, 0][5:6] * y[s]
        if final_norm:
            out = _rms(out) * gf_ref[...]
        y_ref[s] = out


def _ffn(xa, mods, w_up, conv_w, conv_b, w_dn, g_final, *, final_norm):
    B = xa.shape[0]
    rows = xa.shape[1]
    n_t = rows // ROW_TILE
    assert final_norm == (rows == SEQ)
    return pl.pallas_call(
        functools.partial(_ffn_kernel, final_norm=final_norm),
        out_shape=jax.ShapeDtypeStruct((B, rows, D_MODEL), jnp.float32),
        grid=(B // PAIR, n_t),
        in_specs=[
            _prev_halo_spec(D_MODEL), _row_spec(D_MODEL), _next_halo_spec(D_MODEL, rows), _mod_spec(),
            _const_spec(w_up.shape), _const_spec(conv_w.shape), _const_spec(conv_b.shape),
            _const_spec(w_dn.shape), _const_spec(g_final.shape),
        ],
        out_specs=_row_spec(D_MODEL),
        compiler_params=_params(2),
        name="conv_ffn",
    )(xa, xa, xa, mods, w_up, conv_w, conv_b, w_dn, g_final)


def _rope_angles(rope_dim):
    rows = SEQ // GRID_W
    row = jnp.repeat(jnp.arange(rows, dtype=jnp.float32), GRID_W)
    col = jnp.tile(jnp.arange(GRID_W, dtype=jnp.float32), rows)
    n_freq = rope_dim // 4
    freq = ROPE_THETA ** (-jnp.arange(n_freq, dtype=jnp.float32) / n_freq)
    ang = jnp.concatenate([row[:, None] * freq, col[:, None] * freq], axis=-1)
    return jnp.concatenate([ang, jnp.zeros((CTX_LEN, rope_dim // 2), jnp.float32)], axis=0)


def _mla_tables():
    ang = _rope_angles(MLA_ROPE)
    cos, sin = jnp.cos(ang), jnp.sin(ang)
    half = MLA_ROPE // 2
    ones = jnp.ones((T_ALL, MLA_NOPE), jnp.float32)
    zn = jnp.zeros((T_ALL, MLA_NOPE), jnp.float32)
    zh = jnp.zeros((T_ALL, half), jnp.float32)
    zt = jnp.zeros((T_ALL, LANES - MLA_NOPE - MLA_ROPE), jnp.float32)
    c = jnp.concatenate([ones, cos, cos, zt], axis=-1)
    s_up = jnp.concatenate([zn, zh, sin, zt], axis=-1)
    s_dn = jnp.concatenate([zn, -sin, zh, zt], axis=-1)
    return c, s_up, s_dn


def _gqa_tables():
    ang = _rope_angles(GQA_HEAD_DIM)
    cos, sin = jnp.cos(ang), jnp.sin(ang)
    return jnp.concatenate([cos, cos], axis=-1), jnp.concatenate([-sin, sin], axis=-1)


def _mla_head_cols(w, n_heads, nope, rope):
    K = w.shape[0]
    w = w.reshape(K, n_heads, nope + rope)
    parts = [w[..., :nope]]
    if rope:
        parts += [w[..., nope::2], w[..., nope + 1::2]]
    parts.append(jnp.zeros((K, n_heads, LANES - nope - rope), w.dtype))
    return jnp.concatenate(parts, axis=-1).reshape(K, n_heads * LANES)


def _deinterleave(n):
    return np.concatenate([np.arange(0, n, 2), np.arange(1, n, 2)])


def kernel(x, c, ctx, c_ctx, w_mod, b_mod, mix0_w_in, mla_g_q, mla_w_uq, mla_g_kv, mla_w_uk, mla_w_uv,
           pool_w, pool_scale, mix0_w_out, gqa_w_in, gqa_g_q, gqa_g_k, gqa_w_out,
           ffn_w_up, ffn_conv_w, ffn_conv_b, ffn_w_down, g_final):
    B = x.shape[0]
    assert x.shape == (B, SEQ, D_MODEL) and ctx.shape == (B, CTX_LEN, D_MODEL) and B < SUBLANES and B % PAIR == 0 and B % PROJ_PAIR == 0
    assert w_mod.shape[0] == 2

    cc = jnp.zeros((SUBLANES, D_MODEL), jnp.float32).at[:B].set(c).at[B].set(c_ctx)
    mod_all = _modulation(cc, w_mod, b_mod).reshape(2, SUBLANES, N_MOD, D_MODEL)

    def mods_of(i):
        lat = mod_all[i, :B]
        con = jnp.broadcast_to(mod_all[i, B], lat.shape)
        return jnp.stack([lat, con], axis=1)

    w_in = mix0_w_in[0]
    i_kv = MLA_Q_RANK
    i_kr = i_kv + MLA_KV_RANK
    i_p = i_kr + MLA_ROPE
    w_kr = _mla_head_cols(w_in[:, i_kr:i_p], 1, 0, MLA_ROPE)
    w_kr = jnp.roll(w_kr, MLA_NOPE, axis=1)
    w_in0 = _bf(jnp.concatenate([w_in[:, :i_kr], w_kr, w_in[:, i_p:]], axis=1))
    w_uq = _bf(_mla_head_cols(mla_w_uq[0], MLA_HEADS, MLA_NOPE, MLA_ROPE))
    w_uk = _bf(_mla_head_cols(mla_w_uk[0], MLA_HEADS, MLA_NOPE, 0))
    w_uvt = _bf(mla_w_uv[0].T)
    w_out0 = mix0_w_out[0]
    w_oa = _bf(w_out0[:MLA_HEADS * MLA_V])
    w_op = _bf(w_out0[MLA_HEADS * MLA_V:])

    m0 = mods_of(0)
    q, k, vt, p = _in0(x, ctx, m0, w_in0, mla_g_q[0][None], w_uq, mla_g_kv[0][None], w_uk, w_uvt,
                       _mla_tables())
    mla = dict(n_heads=MLA_HEADS_PER_STEP, shared_kv=False, dv=MLA_V)
    o = _attention(q, k, vt, context=False, **mla)
    o_ctx = _attention(q, k, vt, context=True, **mla)
    xa = _out0(x, ctx, m0, o, o_ctx, p, _bf(pool_w[0]), pool_scale[0][None], w_oa, w_op)
    xa = _ffn(xa, m0, _bf(ffn_w_up[0]), ffn_conv_w[0], ffn_conv_b[0][None], _bf(ffn_w_down[0]),
              g_final[None], final_norm=False)

    perm = _deinterleave(GQA_HEAD_DIM)
    n_qk = GQA_HEADS + GQA_KV_HEADS
    w_in = gqa_w_in[0]
    w_qk = w_in[:, :n_qk * LANES].reshape(D_MODEL, n_qk, LANES)[:, :, perm].reshape(D_MODEL, n_qk * LANES)
    w_vt = _bf(w_in[:, n_qk * LANES:].T)
    m1 = mods_of(1)
    q, k, vt = _in1(xa, m1, _bf(w_qk), w_vt, gqa_g_q[0][perm][None], gqa_g_k[0][perm][None], _gqa_tables())
    o = _attention(q, k, vt, n_heads=GQA_GROUP, shared_kv=True, dv=GQA_HEAD_DIM, context=False)
    xa = _out1(xa, m1, o, _bf(gqa_w_out[0]))
    return _ffn(xa, m1, _bf(ffn_w_up[1]), ffn_conv_w[1], ffn_conv_b[1][None], _bf(ffn_w_down[1]),
                g_final[None], final_norm=True)
```

```python
import functools

import jax
import jax.numpy as jnp
import numpy as np
from jax import lax
from jax.experimental import pallas as pl
from jax.experimental.pallas import tpu as pltpu

D_MODEL = 1024
SEQ = 8192
GRID_W = 64
CTX_LEN = 256
T_ALL = SEQ + CTX_LEN
ROPE_THETA = 10000.0
EPS = 1e-6
N_MOD = 6
MLA_HEADS = 8
MLA_NOPE = 64
MLA_ROPE = 32
MLA_V = 64
MLA_Q_RANK = 384
MLA_KV_RANK = 256
POOL_WINDOWS = (2, 4, 8, 16)
POOL_WIDTH = 512
GQA_HEADS = 8
GQA_KV_HEADS = 2
GQA_HEAD_DIM = 128
GQA_GROUP = GQA_HEADS // GQA_KV_HEADS
D_FF = 2816
CONV_W = 3

LANES = 128
SUBLANES = 8
ROW_TILE = 256
N_LAT_TILES = SEQ // ROW_TILE
N_TILES = T_ALL // ROW_TILE
HALO_BLOCKS_PER_TILE = ROW_TILE // SUBLANES
KV_CHUNK = 768
PAIRS_PER_TRIP = 2
TILES_PER_STEP = 4
FF_CHUNK = 256
PAIR = 2
PROJ_PAIR = 4
V7X_VMEM_BYTES = 64 * 1024 * 1024
VMEM_LIMIT = V7X_VMEM_BYTES * 7 // 8

MIX0_IN_PAD = MLA_Q_RANK + MLA_KV_RANK + LANES + POOL_WIDTH
HEADS_W = MLA_HEADS * LANES
LOG2_E = 1.4426950408889634
ONES_ROWS = 16
MLA_VROWS = MLA_V + ONES_ROWS
GQA_VROWS = GQA_HEAD_DIM + ONES_ROWS
MLA_HEADS_PER_STEP = 4


def _params(n_axes):
    return pltpu.CompilerParams(dimension_semantics=("arbitrary",) * n_axes, vmem_limit_bytes=VMEM_LIMIT)


def _rms(x):
    return x * lax.rsqrt(jnp.mean(x * x, axis=-1, keepdims=True) + EPS)


def _dot(a, b):
    return jnp.dot(a, b, preferred_element_type=jnp.float32)


def _dot_nt(a, b):
    return lax.dot_general(a, b, (((1,), (1,)), ((), ())), preferred_element_type=jnp.float32)


def _bf(x):
    return x.astype(jnp.bfloat16)


def _mod_kernel(c_ref, w_ref, b_ref, o_ref):
    s = jax.nn.silu(c_ref[...])
    o_ref[0] = jnp.dot(s, w_ref[0], preferred_element_type=jnp.float32,
                       precision=lax.Precision.HIGHEST) + b_ref[0]


def _modulation(cc, w_mod, b_mod):
    depth = w_mod.shape[0]
    return pl.pallas_call(
        _mod_kernel,
        out_shape=jax.ShapeDtypeStruct((depth, SUBLANES, N_MOD * D_MODEL), jnp.float32),
        grid=(depth, N_MOD),
        in_specs=[
            pl.BlockSpec((SUBLANES, D_MODEL), lambda i, j: (0, 0)),
            pl.BlockSpec((1, D_MODEL, D_MODEL), lambda i, j: (i, 0, j)),
            pl.BlockSpec((1, 1, D_MODEL), lambda i, j: (i, 0, j)),
        ],
        out_specs=pl.BlockSpec((1, SUBLANES, D_MODEL), lambda i, j: (i, 0, j)),
        compiler_params=_params(2),
        name="modulation",
    )(cc, w_mod, b_mod.reshape(depth, 1, N_MOD * D_MODEL))


def _row_spec(width, pair=PAIR):
    return pl.BlockSpec((pair, ROW_TILE, width), lambda b, t: (b, t, 0))


def _mod_spec(pair=PAIR):
    return pl.BlockSpec((pair, 1, N_MOD, D_MODEL), lambda b, t: (b, t // N_LAT_TILES, 0, 0))


def _const_spec(shape):
    nd = len(shape)
    return pl.BlockSpec(shape, lambda b, t: (0,) * nd)


def _table_spec():
    return pl.BlockSpec((ROW_TILE, LANES), lambda b, t: (t, 0))


def _prev_halo_spec(width, pair=PAIR):
    return pl.BlockSpec((pair, SUBLANES, width),
                        lambda b, t: (b, jnp.maximum(t * HALO_BLOCKS_PER_TILE - 1, 0), 0))


def _next_halo_spec(width, total_rows=T_ALL, pair=PAIR):
    last = total_rows // SUBLANES - 1
    return pl.BlockSpec((pair, SUBLANES, width),
                        lambda b, t: (b, jnp.minimum((t + 1) * HALO_BLOCKS_PER_TILE, last), 0))


def _halo_valid(t):
    prev_ok = jnp.logical_and(t != 0, t != N_LAT_TILES)
    next_ok = jnp.logical_and(t != N_LAT_TILES - 1, t != N_LAT_TILES)
    return prev_ok, next_ok


def _with_halo(prev_ref, main_ref, next_ref, s):
    return jnp.concatenate([prev_ref[s], main_ref[s], next_ref[s]], axis=0)


def _halo_row_mask(prev_ok, next_ok):
    r = lax.broadcasted_iota(jnp.int32, (ROW_TILE + 2 * SUBLANES, 1), 0)
    return jnp.logical_and(jnp.logical_or(r >= SUBLANES, prev_ok),
                           jnp.logical_or(r < ROW_TILE + SUBLANES, next_ok))


def _shift_rows(x, k):
    n = x.shape[0]
    shifted = x if k == 0 else pltpu.roll(x, k % n, axis=0)
    return shifted[SUBLANES:SUBLANES + ROW_TILE]


def _rope_mla(x, c, s_up, s_dn):
    return x * c + pltpu.roll(x, MLA_ROPE // 2, axis=1) * s_up + pltpu.roll(x, LANES - MLA_ROPE // 2, axis=1) * s_dn


def _two_stream_specs(width=D_MODEL, pair=PAIR):
    return [pl.BlockSpec((pair, ROW_TILE, width), lambda b, t: (b, jnp.minimum(t, N_LAT_TILES - 1), 0)),
            pl.BlockSpec((pair, CTX_LEN, width), lambda b, t: (b, 0, 0))]


def _two_stream_rows(x_ref, ctx_ref, s):
    return jnp.where(pl.program_id(1) == N_LAT_TILES, ctx_ref[s], x_ref[s])


def _in0_kernel(x_ref, ctx_ref, mod_ref, w_in_ref, gq_ref, w_uq_ref, gkv_ref, w_uk_ref, w_uvt_ref,
                c_ref, su_ref, sd_ref, q_ref, k_ref, v_ref, p_ref):
    i_kv = MLA_Q_RANK
    i_kr = i_kv + MLA_KV_RANK
    i_p = i_kr + LANES
    c, su, sd = c_ref[...], su_ref[...], sd_ref[...]
    scale = (MLA_NOPE + MLA_ROPE) ** -0.5 * LOG2_E
    pair = range(x_ref.shape[0])

    def modulated(s):
        m = mod_ref[s, 0]
        return _bf(_rms(_two_stream_rows(x_ref, ctx_ref, s)) * (1.0 + m[1:2]) + m[0:1])

    ones = jnp.ones((ONES_ROWS, ROW_TILE), jnp.bfloat16)
    for s in pair:
        a = _dot(modulated(s), w_in_ref[...])
        q = _dot(_bf(_rms(a[:, :i_kv]) * gq_ref[...]), w_uq_ref[...])
        ckv = _bf(_rms(a[:, i_kv:i_kr]) * gkv_ref[...])
        kn = _dot(ckv, w_uk_ref[...])
        vt = _dot_nt(w_uvt_ref[...], ckv)
        kr = _rope_mla(a[:, i_kr:i_p], c, su, sd)
        for hd in range(MLA_HEADS):
            sl = slice(hd * LANES, (hd + 1) * LANES)
            q_ref[s, :, sl] = _bf(_rope_mla(q[:, sl], c, su, sd) * scale)
            k_ref[s, :, sl] = _bf(kn[:, sl] + kr)
            v_ref[s, hd * MLA_VROWS:hd * MLA_VROWS + MLA_V, :] = _bf(vt[hd * MLA_V:(hd + 1) * MLA_V])
            v_ref[s, hd * MLA_VROWS + MLA_V:(hd + 1) * MLA_VROWS, :] = ones
        p_ref[s] = a[:, i_p:]


def _vt_spec(rows, pair=PAIR):
    return pl.BlockSpec((pair, rows, ROW_TILE), lambda b, t: (b, 0, t))


def _in0(x, ctx, mods, w_in, g_q, w_uq, g_kv, w_uk, w_uvt, tabs):
    B = x.shape[0]
    assert w_in.shape == (D_MODEL, MIX0_IN_PAD)
    outs = (
        jax.ShapeDtypeStruct((B, T_ALL, HEADS_W), jnp.bfloat16),
        jax.ShapeDtypeStruct((B, T_ALL, HEADS_W), jnp.bfloat16),
        jax.ShapeDtypeStruct((B, MLA_HEADS * MLA_VROWS, T_ALL), jnp.bfloat16),
        jax.ShapeDtypeStruct((B, T_ALL, POOL_WIDTH), jnp.float32),
    )
    return pl.pallas_call(
        _in0_kernel,
        out_shape=outs,
        grid=(B // PROJ_PAIR, N_TILES),
        in_specs=_two_stream_specs(pair=PROJ_PAIR) + [
            _mod_spec(PROJ_PAIR),
            _const_spec(w_in.shape), _const_spec(g_q.shape), _const_spec(w_uq.shape),
            _const_spec(g_kv.shape), _const_spec(w_uk.shape), _const_spec(w_uvt.shape),
            _table_spec(), _table_spec(), _table_spec(),
        ],
        out_specs=(_row_spec(HEADS_W, PROJ_PAIR), _row_spec(HEADS_W, PROJ_PAIR),
                   _vt_spec(MLA_HEADS * MLA_VROWS, PROJ_PAIR), _row_spec(POOL_WIDTH, PROJ_PAIR)),
        compiler_params=_params(2),
        name="mla_pool_in_proj",
    )(x, ctx, mods, w_in, g_q, w_uq, g_kv, w_uk, w_uvt, *tabs)


def _in1_kernel(x_ref, mod_ref, w_in_ref, w_vt_ref, gq_ref, gk_ref, c_ref, s_ref, q_ref, k_ref, v_ref):
    c, sn = c_ref[...], s_ref[...]
    scale = GQA_HEAD_DIM ** -0.5 * LOG2_E
    pair = range(x_ref.shape[0])

    def modulated(s):
        m = mod_ref[s, 0]
        return _bf(_rms(x_ref[s]) * (1.0 + m[1:2]) + m[0:1])

    ones = jnp.ones((ONES_ROWS, ROW_TILE), jnp.bfloat16)

    for s in pair:
        h = modulated(s)
        a = _dot(h, w_in_ref[...])
        vt = _dot_nt(w_vt_ref[...], h)

        def head(j, gain, a=a):
            xn = _rms(a[:, j * LANES:(j + 1) * LANES]) * gain
            return xn * c + pltpu.roll(xn, LANES // 2, axis=1) * sn

        for j in range(GQA_HEADS):
            q_ref[s, :, j * LANES:(j + 1) * LANES] = _bf(head(j, gq_ref[...]) * scale)
        for j in range(GQA_KV_HEADS):
            k_ref[s, :, j * LANES:(j + 1) * LANES] = _bf(head(GQA_HEADS + j, gk_ref[...]))
            v_ref[s, j * GQA_VROWS:j * GQA_VROWS + GQA_HEAD_DIM, :] = _bf(vt[j * GQA_HEAD_DIM:(j + 1) * GQA_HEAD_DIM])
            v_ref[s, j * GQA_VROWS + GQA_HEAD_DIM:(j + 1) * GQA_VROWS, :] = ones


def _in1(xa, mods, w_in, w_vt, g_q, g_k, tabs):
    B = xa.shape[0]
    kv_w = GQA_KV_HEADS * LANES
    outs = (
        jax.ShapeDtypeStruct((B, T_ALL, HEADS_W), jnp.bfloat16),
        jax.ShapeDtypeStruct((B, T_ALL, kv_w), jnp.bfloat16),
        jax.ShapeDtypeStruct((B, GQA_KV_HEADS * GQA_VROWS, T_ALL), jnp.bfloat16),
    )
    return pl.pallas_call(
        _in1_kernel,
        out_shape=outs,
        grid=(B // PAIR, N_TILES),
        in_specs=[
            _row_spec(D_MODEL), _mod_spec(), _const_spec(w_in.shape), _const_spec(w_vt.shape),
            _const_spec(g_q.shape), _const_spec(g_k.shape), _table_spec(), _table_spec(),
        ],
        out_specs=(_row_spec(HEADS_W), _row_spec(kv_w), _vt_spec(GQA_KV_HEADS * GQA_VROWS)),
        compiler_params=_params(2),
        name="gqa_in_proj",
    )(xa, mods, w_in, w_vt, g_q, g_k, *tabs)


def _score(q, k):
    return _dot_nt(k, q)


def _col_max(s):
    return jnp.max(s, axis=0, keepdims=True)


def _accumulate(s, cm, vt, state):
    m, acc = state
    m_new = jnp.maximum(m, cm)
    p = _bf(jnp.exp2(s - m_new))
    return m_new, jnp.exp2(m - m_new) * acc + _dot(vt, p)


def _attn_views(k_ref, vt_ref, *, shared_kv, vrows):
    def k_of(hd, rows):
        j = 0 if shared_kv else hd
        return k_ref[0, rows, j * LANES:(j + 1) * LANES]

    def vt_of(hd, rows):
        j = 0 if shared_kv else hd
        return vt_ref[0, j * vrows:(j + 1) * vrows, rows]

    return k_of, vt_of


def _attn_init(n_heads, vrows, tq):
    return tuple((jnp.full((1, tq), -jnp.inf, jnp.float32), jnp.zeros((vrows, tq), jnp.float32))
                 for _ in range(n_heads))


def _attn_output(states, dv):
    ot = jnp.concatenate([acc[:dv] / acc[dv:dv + 1] for _, acc in states], axis=0)
    return _bf(ot.T)


def _attn_kernel(q_ref, k_ref, vt_ref, o_ref, *s_refs, n_heads, shared_kv, dv):
    vrows = dv + ONES_ROWS
    heads = range(n_heads)
    n_chunks = T_ALL // KV_CHUNK
    assert n_chunks % 2 == 1 and (n_chunks - 3) % (2 * PAIRS_PER_TRIP) == 0
    k_of, vt_of = _attn_views(k_ref, vt_ref, shared_kv=shared_kv, vrows=vrows)

    def chunk(c):
        return pl.ds(pl.multiple_of(c * KV_CHUNK, KV_CHUNK), KV_CHUNK)

    def qs_of(tile):
        return [q_ref[0, tile * ROW_TILE:(tile + 1) * ROW_TILE, hd * LANES:(hd + 1) * LANES] for hd in heads]

    def park(qs, c, hd, buf):
        s = _score(qs[hd], k_of(hd, chunk(c)))
        buf[hd][...] = s
        return _col_max(s)

    def consume(c, hd, buf, cm, state):
        return _accumulate(buf[hd][...], cm, vt_of(hd, chunk(c)), state)

    def half_step(qs, c_next, c, cur, nxt, cms, states):
        new_cms, new_states = [], []
        for hd in heads:
            new_cms.append(park(qs, c_next, hd, nxt))
            new_states.append(consume(c, hd, cur, cms[hd], states[hd]))
        return tuple(new_cms), tuple(new_states)

    even, odd = s_refs[:n_heads], s_refs[n_heads:]
    cms = tuple(park(qs_of(0), 0, hd, even) for hd in heads)
    for tile in range(TILES_PER_STEP):
        qs = qs_of(tile)

        def pair(c, carry, qs=qs, even=even, odd=odd):
            carry = half_step(qs, c + 1, c, odd, even, *carry)
            return half_step(qs, c + 2, c + 1, even, odd, *carry)

        def body(i, carry, pair=pair):
            for k in range(PAIRS_PER_TRIP):
                carry = pair(1 + 2 * (PAIRS_PER_TRIP * i + k), carry)
            return carry

        carry = half_step(qs, 1, 0, even, odd, cms, _attn_init(n_heads, vrows, ROW_TILE))
        carry = lax.fori_loop(0, (n_chunks - 3) // (2 * PAIRS_PER_TRIP), body, carry)
        cms, states = half_step(qs, n_chunks - 1, n_chunks - 2, odd, even, *carry)
        if tile + 1 < TILES_PER_STEP:
            cms, states = half_step(qs_of(tile + 1), 0, n_chunks - 1, even, odd, cms, states)
            even, odd = odd, even
        else:
            states = [consume(n_chunks - 1, hd, even, cms[hd], states[hd]) for hd in heads]
        o_ref[0, tile * ROW_TILE:(tile + 1) * ROW_TILE, :] = _attn_output(states, dv)


def _ctx_attn_kernel(q_ref, k_ref, vt_ref, o_ref, *, n_heads, shared_kv, dv):
    vrows = dv + ONES_ROWS
    k_of, vt_of = _attn_views(k_ref, vt_ref, shared_kv=shared_kv, vrows=vrows)
    rows = slice(None)
    states = []
    for hd, state in enumerate(_attn_init(n_heads, vrows, CTX_LEN)):
        s = _score(q_ref[0, :, hd * LANES:(hd + 1) * LANES], k_of(hd, rows))
        states.append(_accumulate(s, _col_max(s), vt_of(hd, rows), state))
    o_ref[0] = _attn_output(states, dv)


def _attention(q, k, vt, *, n_heads, shared_kv, dv, context):
    B = q.shape[0]
    groups = q.shape[2] // (n_heads * LANES)
    kv_heads = 1 if shared_kv else n_heads
    vrows = dv + ONES_ROWS
    params = dict(n_heads=n_heads, shared_kv=shared_kv, dv=dv)
    if context:
        last = T_ALL // CTX_LEN - 1
        return pl.pallas_call(
            functools.partial(_ctx_attn_kernel, **params),
            out_shape=jax.ShapeDtypeStruct((B, CTX_LEN, groups * n_heads * dv), jnp.bfloat16),
            grid=(B, groups),
            in_specs=[pl.BlockSpec((1, CTX_LEN, n_heads * LANES), lambda b, g: (b, last, g)),
                      pl.BlockSpec((1, CTX_LEN, kv_heads * LANES), lambda b, g: (b, last, g)),
                      pl.BlockSpec((1, kv_heads * vrows, CTX_LEN), lambda b, g: (b, g, last))],
            out_specs=pl.BlockSpec((1, CTX_LEN, n_heads * dv), lambda b, g: (b, 0, g)),
            compiler_params=_params(2),
            name="context_attention",
        )(q, k, vt)
    rows = TILES_PER_STEP * ROW_TILE
    return pl.pallas_call(
        functools.partial(_attn_kernel, **params),
        out_shape=jax.ShapeDtypeStruct((B, SEQ, groups * n_heads * dv), jnp.bfloat16),
        grid=(B, groups, SEQ // rows),
        in_specs=[pl.BlockSpec((1, rows, n_heads * LANES), lambda b, g, i: (b, i, g)),
                  pl.BlockSpec((1, T_ALL, kv_heads * LANES), lambda b, g, i: (b, 0, g)),
                  pl.BlockSpec((1, kv_heads * vrows, T_ALL), lambda b, g, i: (b, g, 0))],
        out_specs=pl.BlockSpec((1, rows, n_heads * dv), lambda b, g, i: (b, i, g)),
        scratch_shapes=[pltpu.VMEM((KV_CHUNK, ROW_TILE), jnp.float32)] * (2 * n_heads),
        compiler_params=_params(3),
        name="attention",
    )(q, k, vt)


def _out0_kernel(x_ref, ctx_ref, mod_ref, o_ref, oc_ref, pp_ref, p_ref, pn_ref, w_pool_ref, s_pool_ref,
                 w_oa_ref, w_op_ref, y_ref):
    t = pl.program_id(1)
    row_ok = _halo_row_mask(*_halo_valid(t))
    t0 = jnp.where(t == N_LAT_TILES, 0, t * ROW_TILE)
    t_len = jnp.where(t == N_LAT_TILES, CTX_LEN, SEQ)
    pos = t0 + lax.broadcasted_iota(jnp.int32, (ROW_TILE, 1), 0)
    pair = range(x_ref.shape[0])
    groups = range(len(POOL_WINDOWS))

    def pool_inputs(s):
        pe = jnp.where(row_ok, _with_halo(pp_ref, p_ref, pn_ref, s), 0.0)
        out = []
        for g, w in enumerate(POOL_WINDOWS):
            run = pe[:, g * LANES:(g + 1) * LANES]
            span = 1
            while span < w:
                run = run + pltpu.roll(run, run.shape[0] - span, axis=0)
                span *= 2
            win = _shift_rows(run, w // 2)
            lo = jnp.clip(pos - w // 2, 0, t_len)
            hi = jnp.clip(pos - w // 2 + w, 0, t_len)
            out.append(_bf(win / (hi - lo).astype(jnp.float32) - p_ref[s, :, g * LANES:(g + 1) * LANES]))
        return out

    d = [pool_inputs(s) for s in pair]
    pooled = [[_dot(d[s][g], w_pool_ref[g]) for s in pair] for g in groups]
    pooled = [_bf(jnp.concatenate([pooled[g][s] for g in groups], axis=-1) * s_pool_ref[...]) for s in pair]
    y = [_dot(_two_stream_rows(o_ref, oc_ref, s), w_oa_ref[...]) + _dot(pooled[s], w_op_ref[...]) for s in pair]
    for s in pair:
        y_ref[s] = _two_stream_rows(x_ref, ctx_ref, s) + mod_ref[s, 0][2:3] * y[s]


def _out0(x, ctx, mods, o, o_ctx, p, w_pool, s_pool, w_oa, w_op):
    B = x.shape[0]
    return pl.pallas_call(
        _out0_kernel,
        out_shape=jax.ShapeDtypeStruct((B, T_ALL, D_MODEL), jnp.float32),
        grid=(B // PROJ_PAIR, N_TILES),
        in_specs=_two_stream_specs(pair=PROJ_PAIR) + [_mod_spec(PROJ_PAIR)]
        + _two_stream_specs(o.shape[2], PROJ_PAIR) + [
            _prev_halo_spec(POOL_WIDTH, PROJ_PAIR), _row_spec(POOL_WIDTH, PROJ_PAIR),
            _next_halo_spec(POOL_WIDTH, pair=PROJ_PAIR),
            _const_spec(w_pool.shape), _const_spec(s_pool.shape),
            _const_spec(w_oa.shape), _const_spec(w_op.shape),
        ],
        out_specs=_row_spec(D_MODEL, PROJ_PAIR),
        compiler_params=_params(2),
        name="mla_pool_out_proj",
    )(x, ctx, mods, o, o_ctx, p, p, p, w_pool, s_pool, w_oa, w_op)


def _out1_kernel(x_ref, mod_ref, o_ref, w_ref, y_ref):
    pair = range(x_ref.shape[0])
    y = [_dot(o_ref[s], w_ref[...]) for s in pair]
    for s in pair:
        y_ref[s] = x_ref[s] + mod_ref[s, 0][2:3] * y[s]


def _out1(xa, mods, o, w_out):
    B = xa.shape[0]
    return pl.pallas_call(
        _out1_kernel,
        out_shape=jax.ShapeDtypeStruct((B, SEQ, D_MODEL), jnp.float32),
        grid=(B // PROJ_PAIR, N_LAT_TILES),
        in_specs=[_row_spec(D_MODEL, PROJ_PAIR), _mod_spec(PROJ_PAIR), _row_spec(HEADS_W, PROJ_PAIR),
                  _const_spec(w_out.shape)],
        out_specs=_row_spec(D_MODEL, PROJ_PAIR),
        compiler_params=_params(2),
        name="gqa_out_proj",
    )(xa, mods, o, w_out)


def _ffn_kernel(xp_ref, x_ref, xn_ref, mod_ref, w_up_ref, cw_ref, cb_ref, w_dn_ref, gf_ref, y_ref,
                *, final_norm):
    row_ok = _halo_row_mask(*_halo_valid(pl.program_id(1)))
    pair = range(x_ref.shape[0])

    def modulated(s):
        m = mod_ref[s, 0]
        he = _rms(_with_halo(xp_ref, x_ref, xn_ref, s)) * (1.0 + m[4:5]) + m[3:4]
        return _bf(jnp.where(row_ok, he, 0.0))

    he = [modulated(s) for s in pair]

    def up(s, cs):
        g = _dot(he[s], w_up_ref[:, cs])
        u = _dot(he[s][SUBLANES:SUBLANES + ROW_TILE], w_up_ref[:, D_FF + cs.start:D_FF + cs.stop])
        return g, u

    def down(cs, g, u):
        acc = cb_ref[:, cs]
        for j in range(CONV_W):
            acc = acc + _shift_rows(g, CONV_W // 2 - j) * cw_ref[j:j + 1, cs]
        return _dot(_bf(jax.nn.silu(acc) * u), w_dn_ref[cs, :])

    chunks = [slice(c0, c0 + FF_CHUNK) for c0 in range(0, D_FF, FF_CHUNK)]
    y = [jnp.zeros((ROW_TILE, D_MODEL), jnp.float32) for _ in pair]
    pending = [up(s, chunks[0]) for s in pair]
    for cs, cs_next in zip(chunks, chunks[1:] + [None]):
        ahead = [up(s, cs_next) for s in pair] if cs_next is not None else None
        y = [y[s] + down(cs, *pending[s]) for s in pair]
        pending = ahead
    for s in pair:
        out = x_ref[s] + mod_ref[s, 0][5:6] * y[s]
        if final_norm:
            out = _rms(out) * gf_ref[...]
        y_ref[s] = out


def _ffn(xa, mods, w_up, conv_w, conv_b, w_dn, g_final, *, final_norm):
    B = xa.shape[0]
    rows = xa.shape[1]
    n_t = rows // ROW_TILE
    assert final_norm == (rows == SEQ) and conv_w.shape == (CONV_W, D_FF)
    return pl.pallas_call(
        functools.partial(_ffn_kernel, final_norm=final_norm),
        out_shape=jax.ShapeDtypeStruct((B, rows, D_MODEL), jnp.float32),
        grid=(B // PAIR, n_t),
        in_specs=[
            _prev_halo_spec(D_MODEL), _row_spec(D_MODEL), _next_halo_spec(D_MODEL, rows), _mod_spec(),
            _const_spec(w_up.shape), _const_spec(conv_w.shape), _const_spec(conv_b.shape),
            _const_spec(w_dn.shape), _const_spec(g_final.shape),
        ],
        out_specs=_row_spec(D_MODEL),
        compiler_params=_params(2),
        name="conv_ffn",
    )(xa, xa, xa, mods, w_up, conv_w, conv_b, w_dn, g_final)


def _rope_angles(rope_dim):
    rows = SEQ // GRID_W
    row = jnp.repeat(jnp.arange(rows, dtype=jnp.float32), GRID_W)
    col = jnp.tile(jnp.arange(GRID_W, dtype=jnp.float32), rows)
    n_freq = rope_dim // 4
    freq = ROPE_THETA ** (-jnp.arange(n_freq, dtype=jnp.float32) / n_freq)
    ang = jnp.concatenate([row[:, None] * freq, col[:, None] * freq], axis=-1)
    return jnp.concatenate([ang, jnp.zeros((CTX_LEN, rope_dim // 2), jnp.float32)], axis=0)


def _mla_tables():
    ang = _rope_angles(MLA_ROPE)
    cos, sin = jnp.cos(ang), jnp.sin(ang)
    half = MLA_ROPE // 2
    ones = jnp.ones((T_ALL, MLA_NOPE), jnp.float32)
    zn = jnp.zeros((T_ALL, MLA_NOPE), jnp.float32)
    zh = jnp.zeros((T_ALL, half), jnp.float32)
    zt = jnp.zeros((T_ALL, LANES - MLA_NOPE - MLA_ROPE), jnp.float32)
    c = jnp.concatenate([ones, cos, cos, zt], axis=-1)
    s_up = jnp.concatenate([zn, zh, sin, zt], axis=-1)
    s_dn = jnp.concatenate([zn, -sin, zh, zt], axis=-1)
    return c, s_up, s_dn


def _gqa_tables():
    ang = _rope_angles(GQA_HEAD_DIM)
    cos, sin = jnp.cos(ang), jnp.sin(ang)
    return jnp.concatenate([cos, cos], axis=-1), jnp.concatenate([-sin, sin], axis=-1)


def _mla_head_cols(w, n_heads, nope, rope):
    K = w.shape[0]
    w = w.reshape(K, n_heads, nope + rope)
    parts = [w[..., :nope]]
    if rope:
        parts += [w[..., nope::2], w[..., nope + 1::2]]
    parts.append(jnp.zeros((K, n_heads, LANES - nope - rope), w.dtype))
    return jnp.concatenate(parts, axis=-1).reshape(K, n_heads * LANES)


def _deinterleave(n):
    return np.concatenate([np.arange(0, n, 2), np.arange(1, n, 2)])


def kernel(x, c, ctx, c_ctx, w_mod, b_mod, mix0_w_in, mla_g_q, mla_w_uq, mla_g_kv, mla_w_uk, mla_w_uv,
           pool_w, pool_scale, mix0_w_out, gqa_w_in, gqa_g_q, gqa_g_k, gqa_w_out,
           ffn_w_up, ffn_conv_w, ffn_conv_b, ffn_w_down, g_final):
    B = x.shape[0]
    assert x.shape == (B, SEQ, D_MODEL) and ctx.shape == (B, CTX_LEN, D_MODEL)
    assert B < SUBLANES and B % PAIR == 0 and B % PROJ_PAIR == 0
    assert w_mod.shape[0] == 2

    cc = jnp.zeros((SUBLANES, D_MODEL), jnp.float32).at[:B].set(c).at[B].set(c_ctx)
    mod_all = _modulation(cc, w_mod, b_mod).reshape(2, SUBLANES, N_MOD, D_MODEL)

    def mods_of(i):
        lat = mod_all[i, :B]
        con = jnp.broadcast_to(mod_all[i, B], lat.shape)
        return jnp.stack([lat, con], axis=1)

    w_in = mix0_w_in[0]
    i_kv = MLA_Q_RANK
    i_kr = i_kv + MLA_KV_RANK
    i_p = i_kr + MLA_ROPE
    w_kr = _mla_head_cols(w_in[:, i_kr:i_p], 1, 0, MLA_ROPE)
    w_kr = jnp.roll(w_kr, MLA_NOPE, axis=1)
    w_in0 = _bf(jnp.concatenate([w_in[:, :i_kr], w_kr, w_in[:, i_p:]], axis=1))
    w_uq = _bf(_mla_head_cols(mla_w_uq[0], MLA_HEADS, MLA_NOPE, MLA_ROPE))
    w_uk = _bf(_mla_head_cols(mla_w_uk[0], MLA_HEADS, MLA_NOPE, 0))
    w_uvt = _bf(mla_w_uv[0].T)
    w_out0 = mix0_w_out[0]
    w_oa = _bf(w_out0[:MLA_HEADS * MLA_V])
    w_op = _bf(w_out0[MLA_HEADS * MLA_V:])

    m0 = mods_of(0)
    q, k, vt, p = _in0(x, ctx, m0, w_in0, mla_g_q[0][None], w_uq, mla_g_kv[0][None], w_uk, w_uvt,
                       _mla_tables())
    mla = dict(n_heads=MLA_HEADS_PER_STEP, shared_kv=False, dv=MLA_V)
    o = _attention(q, k, vt, context=False, **mla)
    o_ctx = _attention(q, k, vt, context=True, **mla)
    xa = _out0(x, ctx, m0, o, o_ctx, p, _bf(pool_w[0]), pool_scale[0][None], w_oa, w_op)
    xa = _ffn(xa, m0, _bf(ffn_w_up[0]), ffn_conv_w[0], ffn_conv_b[0][None], _bf(ffn_w_down[0]),
              g_final[None], final_norm=False)

    perm = _deinterleave(GQA_HEAD_DIM)
    n_qk = GQA_HEADS + GQA_KV_HEADS
    w_in = gqa_w_in[0]
    w_qk = w_in[:, :n_qk * LANES].reshape(D_MODEL, n_qk, LANES)[:, :, perm].reshape(D_MODEL, n_qk * LANES)
    w_vt = _bf(w_in[:, n_qk * LANES:].T)
    m1 = mods_of(1)
    q, k, vt = _in1(xa, m1, _bf(w_qk), w_vt, gqa_g_q[0][perm][None], gqa_g_k[0][perm][None], _gqa_tables())
    o = _attention(q, k, vt, n_heads=GQA_GROUP, shared_kv=True, dv=GQA_HEAD_DIM, context=False)
    xa = _out1(xa, m1, o, _bf(gqa_w_out[0]))
    return _ffn(xa, m1, _bf(ffn_w_up[1]), ffn_conv_w[1], ffn_conv_b[1][None], _bf(ffn_w_down[1]),
                g_final[None], final_norm=True)
```

```python
import functools

import jax
import jax.numpy as jnp
import numpy as np
from jax import lax
from jax.experimental import pallas as pl
from jax.experimental.pallas import tpu as pltpu

D_MODEL = 1024
SEQ = 8192
GRID_W = 64
CTX_LEN = 256
T_ALL = SEQ + CTX_LEN
ROPE_THETA = 10000.0
EPS = 1e-6
N_MOD = 6
MLA_HEADS = 8
MLA_NOPE = 64
MLA_ROPE = 32
MLA_V = 64
MLA_Q_RANK = 384
MLA_KV_RANK = 256
POOL_WINDOWS = (2, 4, 8, 16)
POOL_WIDTH = 512
GQA_HEADS = 8
GQA_KV_HEADS = 2
GQA_HEAD_DIM = 128
GQA_GROUP = GQA_HEADS // GQA_KV_HEADS
D_FF = 2816
CONV_W = 3

LANES = 128
SUBLANES = 8
ROW_TILE = 256
N_LAT_TILES = SEQ // ROW_TILE
N_TILES = T_ALL // ROW_TILE
HALO_BLOCKS_PER_TILE = ROW_TILE // SUBLANES
KV_CHUNK = 768
PAIRS_PER_TRIP = 2
TILES_PER_STEP = 8
FF_CHUNK = 256
PAIR = 2
PROJ_PAIR = 4
V7X_VMEM_BYTES = 64 * 1024 * 1024
VMEM_LIMIT = V7X_VMEM_BYTES * 7 // 8

MIX0_IN_PAD = MLA_Q_RANK + MLA_KV_RANK + LANES + POOL_WIDTH
HEADS_W = MLA_HEADS * LANES
LOG2_E = 1.4426950408889634
ONES_ROWS = 16
MLA_VROWS = MLA_V + ONES_ROWS
GQA_VROWS = GQA_HEAD_DIM + ONES_ROWS
MLA_HEADS_PER_STEP = 4


def _params(n_axes):
    return pltpu.CompilerParams(dimension_semantics=("arbitrary",) * n_axes, vmem_limit_bytes=VMEM_LIMIT)


def _rms(x):
    return x * lax.rsqrt(jnp.mean(x * x, axis=-1, keepdims=True) + EPS)


def _dot(a, b):
    return jnp.dot(a, b, preferred_element_type=jnp.float32)


def _dot_nt(a, b):
    return lax.dot_general(a, b, (((1,), (1,)), ((), ())), preferred_element_type=jnp.float32)


def _bf(x):
    return x.astype(jnp.bfloat16)


def _mod_kernel(c_ref, w_ref, b_ref, o_ref):
    s = jax.nn.silu(c_ref[...])
    o_ref[0] = jnp.dot(s, w_ref[0], preferred_element_type=jnp.float32,
                       precision=lax.Precision.HIGHEST) + b_ref[0]


def _modulation(cc, w_mod, b_mod):
    depth = w_mod.shape[0]
    return pl.pallas_call(
        _mod_kernel,
        out_shape=jax.ShapeDtypeStruct((depth, SUBLANES, N_MOD * D_MODEL), jnp.float32),
        grid=(depth, N_MOD),
        in_specs=[
            pl.BlockSpec((SUBLANES, D_MODEL), lambda i, j: (0, 0)),
            pl.BlockSpec((1, D_MODEL, D_MODEL), lambda i, j: (i, 0, j)),
            pl.BlockSpec((1, 1, D_MODEL), lambda i, j: (i, 0, j)),
        ],
        out_specs=pl.BlockSpec((1, SUBLANES, D_MODEL), lambda i, j: (i, 0, j)),
        compiler_params=_params(2),
        name="modulation",
    )(cc, w_mod, b_mod.reshape(depth, 1, N_MOD * D_MODEL))


def _row_spec(width, pair=PAIR):
    return pl.BlockSpec((pair, ROW_TILE, width), lambda b, t: (b, t, 0))


def _mod_spec(pair=PAIR):
    return pl.BlockSpec((pair, 1, N_MOD, D_MODEL), lambda b, t: (b, t // N_LAT_TILES, 0, 0))


def _const_spec(shape):
    nd = len(shape)
    return pl.BlockSpec(shape, lambda b, t: (0,) * nd)


def _table_spec():
    return pl.BlockSpec((ROW_TILE, LANES), lambda b, t: (t, 0))


def _prev_halo_spec(width, pair=PAIR):
    return pl.BlockSpec((pair, SUBLANES, width),
                        lambda b, t: (b, jnp.maximum(t * HALO_BLOCKS_PER_TILE - 1, 0), 0))


def _next_halo_spec(width, total_rows=T_ALL, pair=PAIR):
    last = total_rows // SUBLANES - 1
    return pl.BlockSpec((pair, SUBLANES, width),
                        lambda b, t: (b, jnp.minimum((t + 1) * HALO_BLOCKS_PER_TILE, last), 0))


def _halo_valid(t):
    prev_ok = jnp.logical_and(t != 0, t != N_LAT_TILES)
    next_ok = jnp.logical_and(t != N_LAT_TILES - 1, t != N_LAT_TILES)
    return prev_ok, next_ok


def _with_halo(prev_ref, main_ref, next_ref, s):
    return jnp.concatenate([prev_ref[s], main_ref[s], next_ref[s]], axis=0)


def _halo_row_mask(prev_ok, next_ok):
    r = lax.broadcasted_iota(jnp.int32, (ROW_TILE + 2 * SUBLANES, 1), 0)
    return jnp.logical_and(jnp.logical_or(r >= SUBLANES, prev_ok),
                           jnp.logical_or(r < ROW_TILE + SUBLANES, next_ok))


def _shift_rows(x, k):
    n = x.shape[0]
    shifted = x if k == 0 else pltpu.roll(x, k % n, axis=0)
    return shifted[SUBLANES:SUBLANES + ROW_TILE]


def _rope_mla(x, c, s_up, s_dn):
    return x * c + pltpu.roll(x, MLA_ROPE // 2, axis=1) * s_up + pltpu.roll(x, LANES - MLA_ROPE // 2, axis=1) * s_dn


def _two_stream_specs(width=D_MODEL, pair=PAIR):
    return [pl.BlockSpec((pair, ROW_TILE, width), lambda b, t: (b, jnp.minimum(t, N_LAT_TILES - 1), 0)),
            pl.BlockSpec((pair, CTX_LEN, width), lambda b, t: (b, 0, 0))]


def _two_stream_rows(x_ref, ctx_ref, s):
    return jnp.where(pl.program_id(1) == N_LAT_TILES, ctx_ref[s], x_ref[s])


def _in0_kernel(x_ref, ctx_ref, mod_ref, w_in_ref, gq_ref, w_uq_ref, gkv_ref, w_uk_ref, w_uvt_ref,
                c_ref, su_ref, sd_ref, q_ref, k_ref, v_ref, p_ref):
    i_kv = MLA_Q_RANK
    i_kr = i_kv + MLA_KV_RANK
    i_p = i_kr + LANES
    c, su, sd = c_ref[...], su_ref[...], sd_ref[...]
    scale = (MLA_NOPE + MLA_ROPE) ** -0.5 * LOG2_E
    pair = range(x_ref.shape[0])

    def modulated(s):
        m = mod_ref[s, 0]
        return _bf(_rms(_two_stream_rows(x_ref, ctx_ref, s)) * (1.0 + m[1:2]) + m[0:1])

    ones = jnp.ones((ONES_ROWS, ROW_TILE), jnp.bfloat16)
    for s in pair:
        a = _dot(modulated(s), w_in_ref[...])
        q = _dot(_bf(_rms(a[:, :i_kv]) * gq_ref[...]), w_uq_ref[...])
        ckv = _bf(_rms(a[:, i_kv:i_kr]) * gkv_ref[...])
        kn = _dot(ckv, w_uk_ref[...])
        vt = _dot_nt(w_uvt_ref[...], ckv)
        kr = _rope_mla(a[:, i_kr:i_p], c, su, sd)
        for hd in range(MLA_HEADS):
            sl = slice(hd * LANES, (hd + 1) * LANES)
            q_ref[s, :, sl] = _bf(_rope_mla(q[:, sl], c, su, sd) * scale)
            k_ref[s, :, sl] = _bf(kn[:, sl] + kr)
            v_ref[s, hd * MLA_VROWS:hd * MLA_VROWS + MLA_V, :] = _bf(vt[hd * MLA_V:(hd + 1) * MLA_V])
            v_ref[s, hd * MLA_VROWS + MLA_V:(hd + 1) * MLA_VROWS, :] = ones
        p_ref[s] = a[:, i_p:]


def _vt_spec(rows, pair=PAIR):
    return pl.BlockSpec((pair, rows, ROW_TILE), lambda b, t: (b, 0, t))


def _in0(x, ctx, mods, w_in, g_q, w_uq, g_kv, w_uk, w_uvt, tabs):
    B = x.shape[0]
    assert w_in.shape == (D_MODEL, MIX0_IN_PAD)
    outs = (
        jax.ShapeDtypeStruct((B, T_ALL, HEADS_W), jnp.bfloat16),
        jax.ShapeDtypeStruct((B, T_ALL, HEADS_W), jnp.bfloat16),
        jax.ShapeDtypeStruct((B, MLA_HEADS * MLA_VROWS, T_ALL), jnp.bfloat16),
        jax.ShapeDtypeStruct((B, T_ALL, POOL_WIDTH), jnp.float32),
    )
    return pl.pallas_call(
        _in0_kernel,
        out_shape=outs,
        grid=(B // PROJ_PAIR, N_TILES),
        in_specs=_two_stream_specs(pair=PROJ_PAIR) + [
            _mod_spec(PROJ_PAIR),
            _const_spec(w_in.shape), _const_spec(g_q.shape), _const_spec(w_uq.shape),
            _const_spec(g_kv.shape), _const_spec(w_uk.shape), _const_spec(w_uvt.shape),
            _table_spec(), _table_spec(), _table_spec(),
        ],
        out_specs=(_row_spec(HEADS_W, PROJ_PAIR), _row_spec(HEADS_W, PROJ_PAIR),
                   _vt_spec(MLA_HEADS * MLA_VROWS, PROJ_PAIR), _row_spec(POOL_WIDTH, PROJ_PAIR)),
        compiler_params=_params(2),
        name="mla_pool_in_proj",
    )(x, ctx, mods, w_in, g_q, w_uq, g_kv, w_uk, w_uvt, *tabs)


def _in1_kernel(x_ref, mod_ref, w_in_ref, w_vt_ref, gq_ref, gk_ref, c_ref, s_ref, q_ref, k_ref, v_ref):
    c, sn = c_ref[...], s_ref[...]
    scale = GQA_HEAD_DIM ** -0.5 * LOG2_E
    pair = range(x_ref.shape[0])

    def modulated(s):
        m = mod_ref[s, 0]
        return _bf(_rms(x_ref[s]) * (1.0 + m[1:2]) + m[0:1])

    ones = jnp.ones((ONES_ROWS, ROW_TILE), jnp.bfloat16)

    for s in pair:
        h = modulated(s)
        a = _dot(h, w_in_ref[...])
        vt = _dot_nt(w_vt_ref[...], h)

        def head(j, gain, a=a):
            xn = _rms(a[:, j * LANES:(j + 1) * LANES]) * gain
            return xn * c + pltpu.roll(xn, LANES // 2, axis=1) * sn

        for j in range(GQA_HEADS):
            q_ref[s, :, j * LANES:(j + 1) * LANES] = _bf(head(j, gq_ref[...]) * scale)
        for j in range(GQA_KV_HEADS):
            k_ref[s, :, j * LANES:(j + 1) * LANES] = _bf(head(GQA_HEADS + j, gk_ref[...]))
            v_ref[s, j * GQA_VROWS:j * GQA_VROWS + GQA_HEAD_DIM, :] = _bf(vt[j * GQA_HEAD_DIM:(j + 1) * GQA_HEAD_DIM])
            v_ref[s, j * GQA_VROWS + GQA_HEAD_DIM:(j + 1) * GQA_VROWS, :] = ones


def _in1(xa, mods, w_in, w_vt, g_q, g_k, tabs):
    B = xa.shape[0]
    kv_w = GQA_KV_HEADS * LANES
    outs = (
        jax.ShapeDtypeStruct((B, T_ALL, HEADS_W), jnp.bfloat16),
        jax.ShapeDtypeStruct((B, T_ALL, kv_w), jnp.bfloat16),
        jax.ShapeDtypeStruct((B, GQA_KV_HEADS * GQA_VROWS, T_ALL), jnp.bfloat16),
    )
    return pl.pallas_call(
        _in1_kernel,
        out_shape=outs,
        grid=(B // PAIR, N_TILES),
        in_specs=[
            _row_spec(D_MODEL), _mod_spec(), _const_spec(w_in.shape), _const_spec(w_vt.shape),
            _const_spec(g_q.shape), _const_spec(g_k.shape), _table_spec(), _table_spec(),
        ],
        out_specs=(_row_spec(HEADS_W), _row_spec(kv_w), _vt_spec(GQA_KV_HEADS * GQA_VROWS)),
        compiler_params=_params(2),
        name="gqa_in_proj",
    )(xa, mods, w_in, w_vt, g_q, g_k, *tabs)


def _score(q, k):
    return _dot_nt(k, q)


def _col_max(s):
    return jnp.max(s, axis=0, keepdims=True)


def _accumulate(s, cm, vt, state):
    m, acc = state
    m_new = jnp.maximum(m, cm)
    p = _bf(jnp.exp2(s - m_new))
    return m_new, jnp.exp2(m - m_new) * acc + _dot(vt, p)


def _attn_views(k_ref, vt_ref, *, shared_kv, vrows):
    def k_of(hd, rows):
        j = 0 if shared_kv else hd
        return k_ref[0, rows, j * LANES:(j + 1) * LANES]

    def vt_of(hd, rows):
        j = 0 if shared_kv else hd
        return vt_ref[0, j * vrows:(j + 1) * vrows, rows]

    return k_of, vt_of


def _attn_init(n_heads, vrows, tq):
    return tuple((jnp.full((1, tq), -jnp.inf, jnp.float32), jnp.zeros((vrows, tq), jnp.float32))
                 for _ in range(n_heads))


def _attn_output(states, dv):
    ot = jnp.concatenate([acc[:dv] / acc[dv:dv + 1] for _, acc in states], axis=0)
    return _bf(ot.T)


def _attn_kernel(q_ref, k_ref, vt_ref, o_ref, *s_refs, n_heads, shared_kv, dv):
    vrows = dv + ONES_ROWS
    heads = range(n_heads)
    n_chunks = T_ALL // KV_CHUNK
    assert n_chunks % 2 == 1 and (n_chunks - 3) % (2 * PAIRS_PER_TRIP) == 0
    k_of, vt_of = _attn_views(k_ref, vt_ref, shared_kv=shared_kv, vrows=vrows)

    def chunk(c):
        return pl.ds(pl.multiple_of(c * KV_CHUNK, KV_CHUNK), KV_CHUNK)

    def qs_of(tile):
        return [q_ref[0, tile * ROW_TILE:(tile + 1) * ROW_TILE, hd * LANES:(hd + 1) * LANES] for hd in heads]

    def park(qs, c, hd, buf):
        s = _score(qs[hd], k_of(hd, chunk(c)))
        buf[hd][...] = s
        return _col_max(s)

    def consume(c, hd, buf, cm, state):
        return _accumulate(buf[hd][...], cm, vt_of(hd, chunk(c)), state)

    def half_step(qs, c_next, c, cur, nxt, cms, states):
        new_cms, new_states = [], []
        for hd in heads:
            new_cms.append(park(qs, c_next, hd, nxt))
            new_states.append(consume(c, hd, cur, cms[hd], states[hd]))
        return tuple(new_cms), tuple(new_states)

    even, odd = s_refs[:n_heads], s_refs[n_heads:]
    cms = tuple(park(qs_of(0), 0, hd, even) for hd in heads)
    for tile in range(TILES_PER_STEP):
        qs = qs_of(tile)

        def pair(c, carry, qs=qs, even=even, odd=odd):
            carry = half_step(qs, c + 1, c, odd, even, *carry)
            return half_step(qs, c + 2, c + 1, even, odd, *carry)

        def body(i, carry, pair=pair):
            for k in range(PAIRS_PER_TRIP):
                carry = pair(1 + 2 * (PAIRS_PER_TRIP * i + k), carry)
            return carry

        carry = half_step(qs, 1, 0, even, odd, cms, _attn_init(n_heads, vrows, ROW_TILE))
        carry = lax.fori_loop(0, (n_chunks - 3) // (2 * PAIRS_PER_TRIP), body, carry)
        cms, states = half_step(qs, n_chunks - 1, n_chunks - 2, odd, even, *carry)
        if tile + 1 < TILES_PER_STEP:
            cms, states = half_step(qs_of(tile + 1), 0, n_chunks - 1, even, odd, cms, states)
            even, odd = odd, even
        else:
            states = [consume(n_chunks - 1, hd, even, cms[hd], states[hd]) for hd in heads]
        o_ref[0, tile * ROW_TILE:(tile + 1) * ROW_TILE, :] = _attn_output(states, dv)


def _ctx_attn_kernel(q_ref, k_ref, vt_ref, o_ref, *, n_heads, shared_kv, dv):
    vrows = dv + ONES_ROWS
    k_of, vt_of = _attn_views(k_ref, vt_ref, shared_kv=shared_kv, vrows=vrows)
    rows = slice(None)
    states = []
    for hd, state in enumerate(_attn_init(n_heads, vrows, CTX_LEN)):
        s = _score(q_ref[0, :, hd * LANES:(hd + 1) * LANES], k_of(hd, rows))
        states.append(_accumulate(s, _col_max(s), vt_of(hd, rows), state))
    o_ref[0] = _attn_output(states, dv)


def _attention(q, k, vt, *, n_heads, shared_kv, dv, context):
    B = q.shape[0]
    groups = q.shape[2] // (n_heads * LANES)
    kv_heads = 1 if shared_kv else n_heads
    vrows = dv + ONES_ROWS
    params = dict(n_heads=n_heads, shared_kv=shared_kv, dv=dv)
    if context:
        last = T_ALL // CTX_LEN - 1
        return pl.pallas_call(
            functools.partial(_ctx_attn_kernel, **params),
            out_shape=jax.ShapeDtypeStruct((B, CTX_LEN, groups * n_heads * dv), jnp.bfloat16),
            grid=(B, groups),
            in_specs=[pl.BlockSpec((1, CTX_LEN, n_heads * LANES), lambda b, g: (b, last, g)),
                      pl.BlockSpec((1, CTX_LEN, kv_heads * LANES), lambda b, g: (b, last, g)),
                      pl.BlockSpec((1, kv_heads * vrows, CTX_LEN), lambda b, g: (b, g, last))],
            out_specs=pl.BlockSpec((1, CTX_LEN, n_heads * dv), lambda b, g: (b, 0, g)),
            compiler_params=_params(2),
            name="context_attention",
        )(q, k, vt)
    rows = TILES_PER_STEP * ROW_TILE
    return pl.pallas_call(
        functools.partial(_attn_kernel, **params),
        out_shape=jax.ShapeDtypeStruct((B, SEQ, groups * n_heads * dv), jnp.bfloat16),
        grid=(B, groups, SEQ // rows),
        in_specs=[pl.BlockSpec((1, rows, n_heads * LANES), lambda b, g, i: (b, i, g)),
                  pl.BlockSpec((1, T_ALL, kv_heads * LANES), lambda b, g, i: (b, 0, g)),
                  pl.BlockSpec((1, kv_heads * vrows, T_ALL), lambda b, g, i: (b, g, 0))],
        out_specs=pl.BlockSpec((1, rows, n_heads * dv), lambda b, g, i: (b, i, g)),
        scratch_shapes=[pltpu.VMEM((KV_CHUNK, ROW_TILE), jnp.float32)] * (2 * n_heads),
        compiler_params=_params(3),
        name="attention",
    )(q, k, vt)


def _out0_kernel(x_ref, ctx_ref, mod_ref, o_ref, oc_ref, pp_ref, p_ref, pn_ref, w_pool_ref, s_pool_ref,
                 w_oa_ref, w_op_ref, y_ref):
    t = pl.program_id(1)
    row_ok = _halo_row_mask(*_halo_valid(t))
    t0 = jnp.where(t == N_LAT_TILES, 0, t * ROW_TILE)
    t_len = jnp.where(t == N_LAT_TILES, CTX_LEN, SEQ)
    pos = t0 + lax.broadcasted_iota(jnp.int32, (ROW_TILE, 1), 0)
    pair = range(x_ref.shape[0])
    groups = range(len(POOL_WINDOWS))

    def pool_inputs(s):
        pe = jnp.where(row_ok, _with_halo(pp_ref, p_ref, pn_ref, s), 0.0)
        out = []
        for g, w in enumerate(POOL_WINDOWS):
            run = pe[:, g * LANES:(g + 1) * LANES]
            span = 1
            while span < w:
                run = run + pltpu.roll(run, run.shape[0] - span, axis=0)
                span *= 2
            win = _shift_rows(run, w // 2)
            lo = jnp.clip(pos - w // 2, 0, t_len)
            hi = jnp.clip(pos - w // 2 + w, 0, t_len)
            out.append(_bf(win / (hi - lo).astype(jnp.float32) - p_ref[s, :, g * LANES:(g + 1) * LANES]))
        return out

    d = [pool_inputs(s) for s in pair]
    pooled = [[_dot(d[s][g], w_pool_ref[g]) for s in pair] for g in groups]
    pooled = [_bf(jnp.concatenate([pooled[g][s] for g in groups], axis=-1) * s_pool_ref[...]) for s in pair]
    y = [_dot(_two_stream_rows(o_ref, oc_ref, s), w_oa_ref[...]) + _dot(pooled[s], w_op_ref[...]) for s in pair]
    for s in pair:
        y_ref[s] = _two_stream_rows(x_ref, ctx_ref, s) + mod_ref[s, 0][2:3] * y[s]


def _out0(x, ctx, mods, o, o_ctx, p, w_pool, s_pool, w_oa, w_op):
    B = x.shape[0]
    return pl.pallas_call(
        _out0_kernel,
        out_shape=jax.ShapeDtypeStruct((B, T_ALL, D_MODEL), jnp.float32),
        grid=(B // PROJ_PAIR, N_TILES),
        in_specs=_two_stream_specs(pair=PROJ_PAIR) + [_mod_spec(PROJ_PAIR)]
        + _two_stream_specs(o.shape[2], PROJ_PAIR) + [
            _prev_halo_spec(POOL_WIDTH, PROJ_PAIR), _row_spec(POOL_WIDTH, PROJ_PAIR),
            _next_halo_spec(POOL_WIDTH, pair=PROJ_PAIR),
            _const_spec(w_pool.shape), _const_spec(s_pool.shape),
            _const_spec(w_oa.shape), _const_spec(w_op.shape),
        ],
        out_specs=_row_spec(D_MODEL, PROJ_PAIR),
        compiler_params=_params(2),
        name="mla_pool_out_proj",
    )(x, ctx, mods, o, o_ctx, p, p, p, w_pool, s_pool, w_oa, w_op)


def _out1_kernel(x_ref, mod_ref, o_ref, w_ref, y_ref):
    pair = range(x_ref.shape[0])
    y = [_dot(o_ref[s], w_ref[...]) for s in pair]
    for s in pair:
        y_ref[s] = x_ref[s] + mod_ref[s, 0][2:3] * y[s]


def _out1(xa, mods, o, w_out):
    B = xa.shape[0]
    return pl.pallas_call(
        _out1_kernel,
        out_shape=jax.ShapeDtypeStruct((B, SEQ, D_MODEL), jnp.float32),
        grid=(B // PROJ_PAIR, N_LAT_TILES),
        in_specs=[_row_spec(D_MODEL, PROJ_PAIR), _mod_spec(PROJ_PAIR), _row_spec(HEADS_W, PROJ_PAIR),
                  _const_spec(w_out.shape)],
        out_specs=_row_spec(D_MODEL, PROJ_PAIR),
        compiler_params=_params(2),
        name="gqa_out_proj",
    )(xa, mods, o, w_out)


def _ffn_kernel(xp_ref, x_ref, xn_ref, mod_ref, w_up_ref, cw_ref, cb_ref, w_dn_ref, gf_ref, y_ref,
                *, final_norm):
    row_ok = _halo_row_mask(*_halo_valid(pl.program_id(1)))
    pair = range(x_ref.shape[0])

    def modulated(s):
        m = mod_ref[s, 0]
        he = _rms(_with_halo(xp_ref, x_ref, xn_ref, s)) * (1.0 + m[4:5]) + m[3:4]
        return _bf(jnp.where(row_ok, he, 0.0))

    he = [modulated(s) for s in pair]

    def up(s, cs):
        g = _dot(he[s], w_up_ref[:, cs])
        u = _dot(he[s][SUBLANES:SUBLANES + ROW_TILE], w_up_ref[:, D_FF + cs.start:D_FF + cs.stop])
        return g, u

    def down(cs, g, u):
        acc = cb_ref[:, cs]
        for j in range(CONV_W):
            acc = acc + _shift_rows(g, CONV_W // 2 - j) * cw_ref[j:j + 1, cs]
        return _dot(_bf(jax.nn.silu(acc) * u), w_dn_ref[cs, :])

    chunks = [slice(c0, c0 + FF_CHUNK) for c0 in range(0, D_FF, FF_CHUNK)]
    y = [jnp.zeros((ROW_TILE, D_MODEL), jnp.float32) for _ in pair]
    pending = [up(s, chunks[0]) for s in pair]
    for cs, cs_next in zip(chunks, chunks[1:] + [None]):
        ahead = [up(s, cs_next) for s in pair] if cs_next is not None else None
        y = [y[s] + down(cs, *pending[s]) for s in pair]
        pending = ahead
    for s in pair:
        out = x_ref[s] + mod_ref[s, 0][5:6] * y[s]
        if final_norm:
            out = _rms(out) * gf_ref[...]
        y_ref[s] = out


def _ffn(xa, mods, w_up, conv_w, conv_b, w_dn, g_final, *, final_norm):
    B = xa.shape[0]
    rows = xa.shape[1]
    n_t = rows // ROW_TILE
    assert final_norm == (rows == SEQ) and conv_w.shape == (CONV_W, D_FF)
    return pl.pallas_call(
        functools.partial(_ffn_kernel, final_norm=final_norm),
        out_shape=jax.ShapeDtypeStruct((B, rows, D_MODEL), jnp.float32),
        grid=(B // PAIR, n_t),
        in_specs=[
            _prev_halo_spec(D_MODEL), _row_spec(D_MODEL), _next_halo_spec(D_MODEL, rows), _mod_spec(),
            _const_spec(w_up.shape), _const_spec(conv_w.shape), _const_spec(conv_b.shape),
            _const_spec(w_dn.shape), _const_spec(g_final.shape),
        ],
        out_specs=_row_spec(D_MODEL),
        compiler_params=_params(2),
        name="conv_ffn",
    )(xa, xa, xa, mods, w_up, conv_w, conv_b, w_dn, g_final)


def _rope_angles(rope_dim):
    rows = SEQ // GRID_W
    row = jnp.repeat(jnp.arange(rows, dtype=jnp.float32), GRID_W)
    col = jnp.tile(jnp.arange(GRID_W, dtype=jnp.float32), rows)
    n_freq = rope_dim // 4
    freq = ROPE_THETA ** (-jnp.arange(n_freq, dtype=jnp.float32) / n_freq)
    ang = jnp.concatenate([row[:, None] * freq, col[:, None] * freq], axis=-1)
    return jnp.concatenate([ang, jnp.zeros((CTX_LEN, rope_dim // 2), jnp.float32)], axis=0)


def _mla_tables():
    ang = _rope_angles(MLA_ROPE)
    cos, sin = jnp.cos(ang), jnp.sin(ang)
    half = MLA_ROPE // 2
    ones = jnp.ones((T_ALL, MLA_NOPE), jnp.float32)
    zn = jnp.zeros((T_ALL, MLA_NOPE), jnp.float32)
    zh = jnp.zeros((T_ALL, half), jnp.float32)
    zt = jnp.zeros((T_ALL, LANES - MLA_NOPE - MLA_ROPE), jnp.float32)
    c = jnp.concatenate([ones, cos, cos, zt], axis=-1)
    s_up = jnp.concatenate([zn, zh, sin, zt], axis=-1)
    s_dn = jnp.concatenate([zn, -sin, zh, zt], axis=-1)
    return c, s_up, s_dn


def _gqa_tables():
    ang = _rope_angles(GQA_HEAD_DIM)
    cos, sin = jnp.cos(ang), jnp.sin(ang)
    return jnp.concatenate([cos, cos], axis=-1), jnp.concatenate([-sin, sin], axis=-1)


def _mla_head_cols(w, n_heads, nope, rope):
    K = w.shape[0]
    w = w.reshape(K, n_heads, nope + rope)
    parts = [w[..., :nope]]
    if rope:
        parts += [w[..., nope::2], w[..., nope + 1::2]]
    parts.append(jnp.zeros((K, n_heads, LANES - nope - rope), w.dtype))
    return jnp.concatenate(parts, axis=-1).reshape(K, n_heads * LANES)


def _deinterleave(n):
    return np.concatenate([np.arange(0, n, 2), np.arange(1, n, 2)])


def kernel(x, c, ctx, c_ctx, w_mod, b_mod, mix0_w_in, mla_g_q, mla_w_uq, mla_g_kv, mla_w_uk, mla_w_uv,
           pool_w, pool_scale, mix0_w_out, gqa_w_in, gqa_g_q, gqa_g_k, gqa_w_out,
           ffn_w_up, ffn_conv_w, ffn_conv_b, ffn_w_down, g_final):
    B = x.shape[0]
    assert x.shape == (B, SEQ, D_MODEL) and ctx.shape == (B, CTX_LEN, D_MODEL)
    assert B < SUBLANES and B % PAIR == 0 and B % PROJ_PAIR == 0
    assert w_mod.shape[0] == 2

    cc = jnp.zeros((SUBLANES, D_MODEL), jnp.float32).at[:B].set(c).at[B].set(c_ctx)
    mod_all = _modulation(cc, w_mod, b_mod).reshape(2, SUBLANES, N_MOD, D_MODEL)

    def mods_of(i):
        lat = mod_all[i, :B]
        con = jnp.broadcast_to(mod_all[i, B], lat.shape)
        return jnp.stack([lat, con], axis=1)

    w_in = mix0_w_in[0]
    i_kv = MLA_Q_RANK
    i_kr = i_kv + MLA_KV_RANK
    i_p = i_kr + MLA_ROPE
    w_kr = _mla_head_cols(w_in[:, i_kr:i_p], 1, 0, MLA_ROPE)
    w_kr = jnp.roll(w_kr, MLA_NOPE, axis=1)
    w_in0 = _bf(jnp.concatenate([w_in[:, :i_kr], w_kr, w_in[:, i_p:]], axis=1))
    w_uq = _bf(_mla_head_cols(mla_w_uq[0], MLA_HEADS, MLA_NOPE, MLA_ROPE))
    w_uk = _bf(_mla_head_cols(mla_w_uk[0], MLA_HEADS, MLA_NOPE, 0))
    w_uvt = _bf(mla_w_uv[0].T)
    w_out0 = mix0_w_out[0]
    w_oa = _bf(w_out0[:MLA_HEADS * MLA_V])
    w_op = _bf(w_out0[MLA_HEADS * MLA_V:])

    m0 = mods_of(0)
    q, k, vt, p = _in0(x, ctx, m0, w_in0, mla_g_q[0][None], w_uq, mla_g_kv[0][None], w_uk, w_uvt,
                       _mla_tables())
    mla = dict(n_heads=MLA_HEADS_PER_STEP, shared_kv=False, dv=MLA_V)
    o = _attention(q, k, vt, context=False, **mla)
    o_ctx = _attention(q, k, vt, context=True, **mla)
    xa = _out0(x, ctx, m0, o, o_ctx, p, _bf(pool_w[0]), pool_scale[0][None], w_oa, w_op)
    xa = _ffn(xa, m0, _bf(ffn_w_up[0]), ffn_conv_w[0], ffn_conv_b[0][None], _bf(ffn_w_down[0]),
              g_final[None], final_norm=False)

    perm = _deinterleave(GQA_HEAD_DIM)
    n_qk = GQA_HEADS + GQA_KV_HEADS
    w_in = gqa_w_in[0]
    w_qk = w_in[:, :n_qk * LANES].reshape(D_MODEL, n_qk, LANES)[:, :, perm].reshape(D_MODEL, n_qk * LANES)
    w_vt = _bf(w_in[:, n_qk * LANES:].T)
    m1 = mods_of(1)
    q, k, vt = _in1(xa, m1, _bf(w_qk), w_vt, gqa_g_q[0][perm][None], gqa_g_k[0][perm][None], _gqa_tables())
    o = _attention(q, k, vt, n_heads=GQA_GROUP, shared_kv=True, dv=GQA_HEAD_DIM, context=False)
    xa = _out1(xa, m1, o, _bf(gqa_w_out[0]))
    return _ffn(xa, m1, _bf(ffn_w_up[1]), ffn_conv_w[1], ffn_conv_b[1][None], _bf(ffn_w_down[1]),
                g_final[None], final_norm=True)
```

```python
import functools

import jax
import jax.numpy as jnp
import numpy as np
from jax import lax
from jax.experimental import pallas as pl
from jax.experimental.pallas import tpu as pltpu

D_MODEL = 1024
SEQ = 8192
GRID_W = 64
CTX_LEN = 256
T_ALL = SEQ + CTX_LEN
ROPE_THETA = 10000.0
EPS = 1e-6
N_MOD = 6
MLA_HEADS = 8
MLA_NOPE = 64
MLA_ROPE = 32
MLA_V = 64
MLA_Q_RANK = 384
MLA_KV_RANK = 256
POOL_WINDOWS = (2, 4, 8, 16)
POOL_WIDTH = 512
GQA_HEADS = 8
GQA_KV_HEADS = 2
GQA_HEAD_DIM = 128
GQA_GROUP = GQA_HEADS // GQA_KV_HEADS
D_FF = 2816
CONV_W = 3

LANES = 128
SUBLANES = 8
ROW_TILE = 256
N_LAT_TILES = SEQ // ROW_TILE
N_TILES = T_ALL // ROW_TILE
HALO_BLOCKS_PER_TILE = ROW_TILE // SUBLANES
KV_CHUNK = 768
PAIRS_PER_TRIP = 2
TILES_PER_STEP = 4
FF_CHUNK = 256
PAIR = 2
PROJ_PAIR = 4
V7X_VMEM_BYTES = 64 * 1024 * 1024
VMEM_LIMIT = V7X_VMEM_BYTES * 7 // 8

MIX0_IN_PAD = MLA_Q_RANK + MLA_KV_RANK + LANES + POOL_WIDTH
HEADS_W = MLA_HEADS * LANES
LOG2_E = 1.4426950408889634
ONES_ROWS = 16
MLA_VROWS = MLA_V + ONES_ROWS
GQA_VROWS = GQA_HEAD_DIM + ONES_ROWS
MLA_HEADS_PER_STEP = 4


def _params(n_axes):
    return pltpu.CompilerParams(dimension_semantics=("arbitrary",) * n_axes, vmem_limit_bytes=VMEM_LIMIT)


def _rms(x):
    return x * lax.rsqrt(jnp.mean(x * x, axis=-1, keepdims=True) + EPS)


def _dot(a, b):
    return jnp.dot(a, b, preferred_element_type=jnp.float32)


def _dot_nt(a, b):
    return lax.dot_general(a, b, (((1,), (1,)), ((), ())), preferred_element_type=jnp.float32)


def _bf(x):
    return x.astype(jnp.bfloat16)


def _mod_kernel(c_ref, w_ref, b_ref, o_ref):
    s = jax.nn.silu(c_ref[...])
    o_ref[0] = jnp.dot(s, w_ref[0], preferred_element_type=jnp.float32,
                       precision=lax.Precision.HIGHEST) + b_ref[0]


def _modulation(cc, w_mod, b_mod):
    depth = w_mod.shape[0]
    return pl.pallas_call(
        _mod_kernel,
        out_shape=jax.ShapeDtypeStruct((depth, SUBLANES, N_MOD * D_MODEL), jnp.float32),
        grid=(depth, N_MOD),
        in_specs=[
            pl.BlockSpec((SUBLANES, D_MODEL), lambda i, j: (0, 0)),
            pl.BlockSpec((1, D_MODEL, D_MODEL), lambda i, j: (i, 0, j)),
            pl.BlockSpec((1, 1, D_MODEL), lambda i, j: (i, 0, j)),
        ],
        out_specs=pl.BlockSpec((1, SUBLANES, D_MODEL), lambda i, j: (i, 0, j)),
        compiler_params=_params(2),
        name="modulation",
    )(cc, w_mod, b_mod.reshape(depth, 1, N_MOD * D_MODEL))


def _row_spec(width, pair=PAIR):
    return pl.BlockSpec((pair, ROW_TILE, width), lambda b, t: (b, t, 0))


def _mod_spec(pair=PAIR):
    return pl.BlockSpec((pair, 1, N_MOD, D_MODEL), lambda b, t: (b, t // N_LAT_TILES, 0, 0))


def _const_spec(shape):
    nd = len(shape)
    return pl.BlockSpec(shape, lambda b, t: (0,) * nd)


def _table_spec():
    return pl.BlockSpec((ROW_TILE, LANES), lambda b, t: (t, 0))


def _prev_halo_spec(width, pair=PAIR):
    return pl.BlockSpec((pair, SUBLANES, width),
                        lambda b, t: (b, jnp.maximum(t * HALO_BLOCKS_PER_TILE - 1, 0), 0))


def _next_halo_spec(width, total_rows=T_ALL, pair=PAIR):
    last = total_rows // SUBLANES - 1
    return pl.BlockSpec((pair, SUBLANES, width),
                        lambda b, t: (b, jnp.minimum((t + 1) * HALO_BLOCKS_PER_TILE, last), 0))


def _halo_valid(t):
    prev_ok = jnp.logical_and(t != 0, t != N_LAT_TILES)
    next_ok = jnp.logical_and(t != N_LAT_TILES - 1, t != N_LAT_TILES)
    return prev_ok, next_ok


def _with_halo(prev_ref, main_ref, next_ref, s):
    return jnp.concatenate([prev_ref[s], main_ref[s], next_ref[s]], axis=0)


def _halo_row_mask(prev_ok, next_ok):
    r = lax.broadcasted_iota(jnp.int32, (ROW_TILE + 2 * SUBLANES, 1), 0)
    return jnp.logical_and(jnp.logical_or(r >= SUBLANES, prev_ok),
                           jnp.logical_or(r < ROW_TILE + SUBLANES, next_ok))


def _shift_rows(x, k):
    n = x.shape[0]
    shifted = x if k == 0 else pltpu.roll(x, k % n, axis=0)
    return shifted[SUBLANES:SUBLANES + ROW_TILE]


def _rope_mla(x, c, s_up, s_dn):
    return x * c + pltpu.roll(x, MLA_ROPE // 2, axis=1) * s_up + pltpu.roll(x, LANES - MLA_ROPE // 2, axis=1) * s_dn


def _two_stream_specs(width=D_MODEL, pair=PAIR):
    return [pl.BlockSpec((pair, ROW_TILE, width), lambda b, t: (b, jnp.minimum(t, N_LAT_TILES - 1), 0)),
            pl.BlockSpec((pair, CTX_LEN, width), lambda b, t: (b, 0, 0))]


def _two_stream_rows(x_ref, ctx_ref, s):
    return jnp.where(pl.program_id(1) == N_LAT_TILES, ctx_ref[s], x_ref[s])


def _in0_kernel(x_ref, ctx_ref, mod_ref, w_in_ref, gq_ref, w_uq_ref, gkv_ref, w_uk_ref, w_uvt_ref,
                c_ref, su_ref, sd_ref, q_ref, k_ref, v_ref, p_ref):
    i_kv = MLA_Q_RANK
    i_kr = i_kv + MLA_KV_RANK
    i_p = i_kr + LANES
    c, su, sd = c_ref[...], su_ref[...], sd_ref[...]
    scale = (MLA_NOPE + MLA_ROPE) ** -0.5 * LOG2_E
    pair = range(x_ref.shape[0])

    def modulated(s):
        m = mod_ref[s, 0]
        return _bf(_rms(_two_stream_rows(x_ref, ctx_ref, s)) * (1.0 + m[1:2]) + m[0:1])

    ones = jnp.ones((ONES_ROWS, ROW_TILE), jnp.bfloat16)
    for s in pair:
        a = _dot(modulated(s), w_in_ref[...])
        q = _dot(_bf(_rms(a[:, :i_kv]) * gq_ref[...]), w_uq_ref[...])
        ckv = _bf(_rms(a[:, i_kv:i_kr]) * gkv_ref[...])
        kn = _dot(ckv, w_uk_ref[...])
        vt = _dot_nt(w_uvt_ref[...], ckv)
        kr = _rope_mla(a[:, i_kr:i_p], c, su, sd)
        for hd in range(MLA_HEADS):
            sl = slice(hd * LANES, (hd + 1) * LANES)
            q_ref[s, :, sl] = _bf(_rope_mla(q[:, sl], c, su, sd) * scale)
            k_ref[s, :, sl] = _bf(kn[:, sl] + kr)
            v_ref[s, hd * MLA_VROWS:hd * MLA_VROWS + MLA_V, :] = _bf(vt[hd * MLA_V:(hd + 1) * MLA_V])
            v_ref[s, hd * MLA_VROWS + MLA_V:(hd + 1) * MLA_VROWS, :] = ones
        p_ref[s] = a[:, i_p:]


def _vt_spec(rows, pair=PAIR):
    return pl.BlockSpec((pair, rows, ROW_TILE), lambda b, t: (b, 0, t))


def _in0(x, ctx, mods, w_in, g_q, w_uq, g_kv, w_uk, w_uvt, tabs):
    B = x.shape[0]
    assert w_in.shape == (D_MODEL, MIX0_IN_PAD)
    outs = (
        jax.ShapeDtypeStruct((B, T_ALL, HEADS_W), jnp.bfloat16),
        jax.ShapeDtypeStruct((B, T_ALL, HEADS_W), jnp.bfloat16),
        jax.ShapeDtypeStruct((B, MLA_HEADS * MLA_VROWS, T_ALL), jnp.bfloat16),
        jax.ShapeDtypeStruct((B, T_ALL, POOL_WIDTH), jnp.float32),
    )
    return pl.pallas_call(
        _in0_kernel,
        out_shape=outs,
        grid=(B // PROJ_PAIR, N_TILES),
        in_specs=_two_stream_specs(pair=PROJ_PAIR) + [
            _mod_spec(PROJ_PAIR),
            _const_spec(w_in.shape), _const_spec(g_q.shape), _const_spec(w_uq.shape),
            _const_spec(g_kv.shape), _const_spec(w_uk.shape), _const_spec(w_uvt.shape),
            _table_spec(), _table_spec(), _table_spec(),
        ],
        out_specs=(_row_spec(HEADS_W, PROJ_PAIR), _row_spec(HEADS_W, PROJ_PAIR),
                   _vt_spec(MLA_HEADS * MLA_VROWS, PROJ_PAIR), _row_spec(POOL_WIDTH, PROJ_PAIR)),
        compiler_params=_params(2),
        name="mla_pool_in_proj",
    )(x, ctx, mods, w_in, g_q, w_uq, g_kv, w_uk, w_uvt, *tabs)


def _in1_kernel(x_ref, mod_ref, w_in_ref, w_vt_ref, gq_ref, gk_ref, c_ref, s_ref, q_ref, k_ref, v_ref):
    c, sn = c_ref[...], s_ref[...]
    scale = GQA_HEAD_DIM ** -0.5 * LOG2_E
    pair = range(x_ref.shape[0])

    def modulated(s):
        m = mod_ref[s, 0]
        return _bf(_rms(x_ref[s]) * (1.0 + m[1:2]) + m[0:1])

    ones = jnp.ones((ONES_ROWS, ROW_TILE), jnp.bfloat16)

    for s in pair:
        h = modulated(s)
        a = _dot(h, w_in_ref[...])
        vt = _dot_nt(w_vt_ref[...], h)

        def head(j, gain, a=a):
            xn = _rms(a[:, j * LANES:(j + 1) * LANES]) * gain
            return xn * c + pltpu.roll(xn, LANES // 2, axis=1) * sn

        for j in range(GQA_HEADS):
            q_ref[s, :, j * LANES:(j + 1) * LANES] = _bf(head(j, gq_ref[...]) * scale)
        for j in range(GQA_KV_HEADS):
            k_ref[s, :, j * LANES:(j + 1) * LANES] = _bf(head(GQA_HEADS + j, gk_ref[...]))
            v_ref[s, j * GQA_VROWS:j * GQA_VROWS + GQA_HEAD_DIM, :] = _bf(vt[j * GQA_HEAD_DIM:(j + 1) * GQA_HEAD_DIM])
            v_ref[s, j * GQA_VROWS + GQA_HEAD_DIM:(j + 1) * GQA_VROWS, :] = ones


def _in1(xa, mods, w_in, w_vt, g_q, g_k, tabs):
    B = xa.shape[0]
    kv_w = GQA_KV_HEADS * LANES
    outs = (
        jax.ShapeDtypeStruct((B, T_ALL, HEADS_W), jnp.bfloat16),
        jax.ShapeDtypeStruct((B, T_ALL, kv_w), jnp.bfloat16),
        jax.ShapeDtypeStruct((B, GQA_KV_HEADS * GQA_VROWS, T_ALL), jnp.bfloat16),
    )
    return pl.pallas_call(
        _in1_kernel,
        out_shape=outs,
        grid=(B // PAIR, N_TILES),
        in_specs=[
            _row_spec(D_MODEL), _mod_spec(), _const_spec(w_in.shape), _const_spec(w_vt.shape),
            _const_spec(g_q.shape), _const_spec(g_k.shape), _table_spec(), _table_spec(),
        ],
        out_specs=(_row_spec(HEADS_W), _row_spec(kv_w), _vt_spec(GQA_KV_HEADS * GQA_VROWS)),
        compiler_params=_params(2),
        name="gqa_in_proj",
    )(xa, mods, w_in, w_vt, g_q, g_k, *tabs)


def _score(q, k):
    return _dot_nt(k, q)


def _col_max(s):
    return jnp.max(s, axis=0, keepdims=True)


def _accumulate(s, cm, vt, state):
    m, acc = state
    m_new = jnp.maximum(m, cm)
    p = _bf(jnp.exp2(s - m_new))
    return m_new, jnp.exp2(m - m_new) * acc + _dot(vt, p)


def _attn_views(k_ref, vt_ref, *, shared_kv, vrows):
    def k_of(hd, rows):
        j = 0 if shared_kv else hd
        return k_ref[0, rows, j * LANES:(j + 1) * LANES]

    def vt_of(hd, rows):
        j = 0 if shared_kv else hd
        return vt_ref[0, j * vrows:(j + 1) * vrows, rows]

    return k_of, vt_of


def _attn_init(n_heads, vrows, tq):
    return tuple((jnp.full((1, tq), -jnp.inf, jnp.float32), jnp.zeros((vrows, tq), jnp.float32))
                 for _ in range(n_heads))


def _attn_output(states, dv):
    ot = jnp.concatenate([acc[:dv] / acc[dv:dv + 1] for _, acc in states], axis=0)
    return _bf(ot.T)


def _attn_kernel(q_ref, k_ref, vt_ref, o_ref, *s_refs, n_heads, shared_kv, dv):
    vrows = dv + ONES_ROWS
    heads = range(n_heads)
    n_chunks = T_ALL // KV_CHUNK
    assert n_chunks % 2 == 1 and (n_chunks - 3) % (2 * PAIRS_PER_TRIP) == 0
    k_of, vt_of = _attn_views(k_ref, vt_ref, shared_kv=shared_kv, vrows=vrows)

    def chunk(c):
        return pl.ds(pl.multiple_of(c * KV_CHUNK, KV_CHUNK), KV_CHUNK)

    def qs_of(tile):
        return [q_ref[0, tile * ROW_TILE:(tile + 1) * ROW_TILE, hd * LANES:(hd + 1) * LANES] for hd in heads]

    def park(qs, c, hd, buf):
        s = _score(qs[hd], k_of(hd, chunk(c)))
        buf[hd][...] = s
        return _col_max(s)

    def consume(c, hd, buf, cm, state):
        return _accumulate(buf[hd][...], cm, vt_of(hd, chunk(c)), state)

    def half_step(qs, c_next, c, cur, nxt, cms, states):
        new_cms, new_states = [], []
        for hd in heads:
            new_cms.append(park(qs, c_next, hd, nxt))
            new_states.append(consume(c, hd, cur, cms[hd], states[hd]))
        return tuple(new_cms), tuple(new_states)

    even, odd = s_refs[:n_heads], s_refs[n_heads:]
    cms = tuple(park(qs_of(0), 0, hd, even) for hd in heads)
    for tile in range(TILES_PER_STEP):
        qs = qs_of(tile)

        def pair(c, carry, qs=qs, even=even, odd=odd):
            carry = half_step(qs, c + 1, c, odd, even, *carry)
            return half_step(qs, c + 2, c + 1, even, odd, *carry)

        def body(i, carry, pair=pair):
            for k in range(PAIRS_PER_TRIP):
                carry = pair(1 + 2 * (PAIRS_PER_TRIP * i + k), carry)
            return carry

        carry = half_step(qs, 1, 0, even, odd, cms, _attn_init(n_heads, vrows, ROW_TILE))
        carry = lax.fori_loop(0, (n_chunks - 3) // (2 * PAIRS_PER_TRIP), body, carry)
        cms, states = half_step(qs, n_chunks - 1, n_chunks - 2, odd, even, *carry)
        if tile + 1 < TILES_PER_STEP:
            cms, states = half_step(qs_of(tile + 1), 0, n_chunks - 1, even, odd, cms, states)
            even, odd = odd, even
        else:
            states = [consume(n_chunks - 1, hd, even, cms[hd], states[hd]) for hd in heads]
        o_ref[0, tile * ROW_TILE:(tile + 1) * ROW_TILE, :] = _attn_output(states, dv)


def _ctx_attn_kernel(q_ref, k_ref, vt_ref, o_ref, *, n_heads, shared_kv, dv):
    vrows = dv + ONES_ROWS
    k_of, vt_of = _attn_views(k_ref, vt_ref, shared_kv=shared_kv, vrows=vrows)
    rows = slice(None)
    states = []
    for hd, state in enumerate(_attn_init(n_heads, vrows, CTX_LEN)):
        s = _score(q_ref[0, :, hd * LANES:(hd + 1) * LANES], k_of(hd, rows))
        states.append(_accumulate(s, _col_max(s), vt_of(hd, rows), state))
    o_ref[0] = _attn_output(states, dv)


def _attention(q, k, vt, *, n_heads, shared_kv, dv, context):
    B = q.shape[0]
    groups = q.shape[2] // (n_heads * LANES)
    kv_heads = 1 if shared_kv else n_heads
    vrows = dv + ONES_ROWS
    params = dict(n_heads=n_heads, shared_kv=shared_kv, dv=dv)
    if context:
        last = T_ALL // CTX_LEN - 1
        return pl.pallas_call(
            functools.partial(_ctx_attn_kernel, **params),
            out_shape=jax.ShapeDtypeStruct((B, CTX_LEN, groups * n_heads * dv), jnp.bfloat16),
            grid=(B, groups),
            in_specs=[pl.BlockSpec((1, CTX_LEN, n_heads * LANES), lambda b, g: (b, last, g)),
                      pl.BlockSpec((1, CTX_LEN, kv_heads * LANES), lambda b, g: (b, last, g)),
                      pl.BlockSpec((1, kv_heads * vrows, CTX_LEN), lambda b, g: (b, g, last))],
            out_specs=pl.BlockSpec((1, CTX_LEN, n_heads * dv), lambda b, g: (b, 0, g)),
            compiler_params=_params(2),
            name="context_attention",
        )(q, k, vt)
    rows = TILES_PER_STEP * ROW_TILE
    return pl.pallas_call(
        functools.partial(_attn_kernel, **params),
        out_shape=jax.ShapeDtypeStruct((B, SEQ, groups * n_heads * dv), jnp.bfloat16),
        grid=(B, groups, SEQ // rows),
        in_specs=[pl.BlockSpec((1, rows, n_heads * LANES), lambda b, g, i: (b, i, g)),
                  pl.BlockSpec((1, T_ALL, kv_heads * LANES), lambda b, g, i: (b, 0, g)),
                  pl.BlockSpec((1, kv_heads * vrows, T_ALL), lambda b, g, i: (b, g, 0))],
        out_specs=pl.BlockSpec((1, rows, n_heads * dv), lambda b, g, i: (b, i, g)),
        scratch_shapes=[pltpu.VMEM((KV_CHUNK, ROW_TILE), jnp.float32)] * (2 * n_heads),
        compiler_params=_params(3),
        name="attention",
    )(q, k, vt)


def _out0_kernel(x_ref, ctx_ref, mod_ref, o_ref, oc_ref, pp_ref, p_ref, pn_ref, w_pool_ref, s_pool_ref,
                 w_oa_ref, w_op_ref, y_ref):
    t = pl.program_id(1)
    row_ok = _halo_row_mask(*_halo_valid(t))
    t0 = jnp.where(t == N_LAT_TILES, 0, t * ROW_TILE)
    t_len = jnp.where(t == N_LAT_TILES, CTX_LEN, SEQ)
    pos = t0 + lax.broadcasted_iota(jnp.int32, (ROW_TILE, 1), 0)
    pair = range(x_ref.shape[0])
    groups = range(len(POOL_WINDOWS))

    def pool_inputs(s):
        pe = jnp.where(row_ok, _with_halo(pp_ref, p_ref, pn_ref, s), 0.0)
        out = []
        for g, w in enumerate(POOL_WINDOWS):
            run = pe[:, g * LANES:(g + 1) * LANES]
            span = 1
            while span < w:
                run = run + pltpu.roll(run, run.shape[0] - span, axis=0)
                span *= 2
            win = _shift_rows(run, w // 2)
            lo = jnp.clip(pos - w // 2, 0, t_len)
            hi = jnp.clip(pos - w // 2 + w, 0, t_len)
            out.append(_bf(win / (hi - lo).astype(jnp.float32) - p_ref[s, :, g * LANES:(g + 1) * LANES]))
        return out

    d = [pool_inputs(s) for s in pair]
    pooled = [[_dot(d[s][g], w_pool_ref[g]) for s in pair] for g in groups]
    pooled = [_bf(jnp.concatenate([pooled[g][s] for g in groups], axis=-1) * s_pool_ref[...]) for s in pair]
    y = [_dot(_two_stream_rows(o_ref, oc_ref, s), w_oa_ref[...]) + _dot(pooled[s], w_op_ref[...]) for s in pair]
    for s in pair:
        y_ref[s] = _two_stream_rows(x_ref, ctx_ref, s) + mod_ref[s, 0][2:3] * y[s]


def _out0(x, ctx, mods, o, o_ctx, p, w_pool, s_pool, w_oa, w_op):
    B = x.shape[0]
    return pl.pallas_call(
        _out0_kernel,
        out_shape=jax.ShapeDtypeStruct((B, T_ALL, D_MODEL), jnp.float32),
        grid=(B // PROJ_PAIR, N_TILES),
        in_specs=_two_stream_specs(pair=PROJ_PAIR) + [_mod_spec(PROJ_PAIR)]
        + _two_stream_specs(o.shape[2], PROJ_PAIR) + [
            _prev_halo_spec(POOL_WIDTH, PROJ_PAIR), _row_spec(POOL_WIDTH, PROJ_PAIR),
            _next_halo_spec(POOL_WIDTH, pair=PROJ_PAIR),
            _const_spec(w_pool.shape), _const_spec(s_pool.shape),
            _const_spec(w_oa.shape), _const_spec(w_op.shape),
        ],
        out_specs=_row_spec(D_MODEL, PROJ_PAIR),
        compiler_params=_params(2),
        name="mla_pool_out_proj",
    )(x, ctx, mods, o, o_ctx, p, p, p, w_pool, s_pool, w_oa, w_op)


def _out1_kernel(x_ref, mod_ref, o_ref, w_ref, y_ref):
    pair = range(x_ref.shape[0])
    y = [_dot(o_ref[s], w_ref[...]) for s in pair]
    for s in pair:
        y_ref[s] = x_ref[s] + mod_ref[s, 0][2:3] * y[s]


def _out1(xa, mods, o, w_out):
    B = xa.shape[0]
    return pl.pallas_call(
        _out1_kernel,
        out_shape=jax.ShapeDtypeStruct((B, SEQ, D_MODEL), jnp.float32),
        grid=(B // PROJ_PAIR, N_LAT_TILES),
        in_specs=[_row_spec(D_MODEL, PROJ_PAIR), _mod_spec(PROJ_PAIR), _row_spec(HEADS_W, PROJ_PAIR),
                  _const_spec(w_out.shape)],
        out_specs=_row_spec(D_MODEL, PROJ_PAIR),
        compiler_params=_params(2),
        name="gqa_out_proj",
    )(xa, mods, o, w_out)


def _ffn_kernel(xp_ref, x_ref, xn_ref, mod_ref, w_up_ref, cw_ref, cb_ref, w_dn_ref, gf_ref, y_ref,
                *, final_norm):
    row_ok = _halo_row_mask(*_halo_valid(pl.program_id(1)))
    pair = range(x_ref.shape[0])

    def modulated(s):
        m = mod_ref[s, 0]
        he = _rms(_with_halo(xp_ref, x_ref, xn_ref, s)) * (1.0 + m[4:5]) + m[3:4]
        return _bf(jnp.where(row_ok, he, 0.0))

    he = [modulated(s) for s in pair]

    def up(s, cs):
        g = _dot(he[s], w_up_ref[:, cs])
        u = _dot(he[s][SUBLANES:SUBLANES + ROW_TILE], w_up_ref[:, D_FF + cs.start:D_FF + cs.stop])
        return g, u

    def down(cs, g, u):
        acc = cb_ref[:, cs]
        for j in range(CONV_W):
            acc = acc + _shift_rows(g, CONV_W // 2 - j) * cw_ref[j:j + 1, cs]
        return _dot(_bf(jax.nn.silu(acc) * u), w_dn_ref[cs, :])

    chunks = [slice(c0, c0 + FF_CHUNK) for c0 in range(0, D_FF, FF_CHUNK)]
    y = [jnp.zeros((ROW_TILE, D_MODEL), jnp.float32) for _ in pair]
    pending = [up(s, chunks[0]) for s in pair]
    for cs, cs_next in zip(chunks, chunks[1:] + [None]):
        ahead = [up(s, cs_next) for s in pair] if cs_next is not None else None
        y = [y[s] + down(cs, *pending[s]) for s in pair]
        pending = ahead
    for s in pair:
        out = x_ref[s] + mod_ref[s, 0][5:6] * y[s]
        if final_norm:
            out = _rms(out) * gf_ref[...]
        y_ref[s] = out


def _ffn(xa, mods, w_up, conv_w, conv_b, w_dn, g_final, *, final_norm):
    B = xa.shape[0]
    rows = xa.shape[1]
    n_t = rows // ROW_TILE
    assert final_norm == (rows == SEQ) and conv_w.shape == (CONV_W, D_FF)
    return pl.pallas_call(
        functools.partial(_ffn_kernel, final_norm=final_norm),
        out_shape=jax.ShapeDtypeStruct((B, rows, D_MODEL), jnp.float32),
        grid=(B // PAIR, n_t),
        in_specs=[
            _prev_halo_spec(D_MODEL), _row_spec(D_MODEL), _next_halo_spec(D_MODEL, rows), _mod_spec(),
            _const_spec(w_up.shape), _const_spec(conv_w.shape), _const_spec(conv_b.shape),
            _const_spec(w_dn.shape), _const_spec(g_final.shape),
        ],
        out_specs=_row_spec(D_MODEL),
        compiler_params=_params(2),
        name="conv_ffn",
    )(xa, xa, xa, mods, w_up, conv_w, conv_b, w_dn, g_final)


def _rope_angles(rope_dim):
    rows = SEQ // GRID_W
    row = np.repeat(np.arange(rows, dtype=np.float32), GRID_W)
    col = np.tile(np.arange(GRID_W, dtype=np.float32), rows)
    n_freq = rope_dim // 4
    freq = (np.float32(ROPE_THETA) ** (-np.arange(n_freq, dtype=np.float32) / n_freq)).astype(np.float32)
    ang = np.concatenate([row[:, None] * freq, col[:, None] * freq], axis=-1)
    return np.concatenate([ang, np.zeros((CTX_LEN, rope_dim // 2), np.float32)], axis=0).astype(np.float32)


def _mla_tables():
    ang = _rope_angles(MLA_ROPE)
    cos, sin = np.cos(ang), np.sin(ang)
    half = MLA_ROPE // 2
    ones = np.ones((T_ALL, MLA_NOPE), np.float32)
    zn = np.zeros((T_ALL, MLA_NOPE), np.float32)
    zh = np.zeros((T_ALL, half), np.float32)
    zt = np.zeros((T_ALL, LANES - MLA_NOPE - MLA_ROPE), np.float32)
    c = np.concatenate([ones, cos, cos, zt], axis=-1)
    s_up = np.concatenate([zn, zh, sin, zt], axis=-1)
    s_dn = np.concatenate([zn, -sin, zh, zt], axis=-1)
    return jnp.asarray(c), jnp.asarray(s_up), jnp.asarray(s_dn)


def _gqa_tables():
    ang = _rope_angles(GQA_HEAD_DIM)
    cos, sin = np.cos(ang), np.sin(ang)
    return jnp.asarray(np.concatenate([cos, cos], axis=-1)), jnp.asarray(np.concatenate([-sin, sin], axis=-1))


def _mla_head_cols(w, n_heads, nope, rope):
    K = w.shape[0]
    w = w.reshape(K, n_heads, nope + rope)
    parts = [w[..., :nope]]
    if rope:
        parts += [w[..., nope::2], w[..., nope + 1::2]]
    parts.append(jnp.zeros((K, n_heads, LANES - nope - rope), w.dtype))
    return jnp.concatenate(parts, axis=-1).reshape(K, n_heads * LANES)


def _deinterleave(n):
    return np.concatenate([np.arange(0, n, 2), np.arange(1, n, 2)])


def kernel(x, c, ctx, c_ctx, w_mod, b_mod, mix0_w_in, mla_g_q, mla_w_uq, mla_g_kv, mla_w_uk, mla_w_uv,
           pool_w, pool_scale, mix0_w_out, gqa_w_in, gqa_g_q, gqa_g_k, gqa_w_out,
           ffn_w_up, ffn_conv_w, ffn_conv_b, ffn_w_down, g_final):
    B = x.shape[0]
    assert x.shape == (B, SEQ, D_MODEL) and ctx.shape == (B, CTX_LEN, D_MODEL)
    assert B < SUBLANES and B % PAIR == 0 and B % PROJ_PAIR == 0
    assert w_mod.shape[0] == 2

    cc = jnp.zeros((SUBLANES, D_MODEL), jnp.float32).at[:B].set(c).at[B].set(c_ctx)
    mod_all = _modulation(cc, w_mod, b_mod).reshape(2, SUBLANES, N_MOD, D_MODEL)

    def mods_of(i):
        lat = mod_all[i, :B]
        con = jnp.broadcast_to(mod_all[i, B], lat.shape)
        return jnp.stack([lat, con], axis=1)

    w_in = mix0_w_in[0]
    i_kv = MLA_Q_RANK
    i_kr = i_kv + MLA_KV_RANK
    i_p = i_kr + MLA_ROPE
    w_kr = _mla_head_cols(w_in[:, i_kr:i_p], 1, 0, MLA_ROPE)
    w_kr = jnp.roll(w_kr, MLA_NOPE, axis=1)
    w_in0 = _bf(jnp.concatenate([w_in[:, :i_kr], w_kr, w_in[:, i_p:]], axis=1))
    w_uq = _bf(_mla_head_cols(mla_w_uq[0], MLA_HEADS, MLA_NOPE, MLA_ROPE))
    w_uk = _bf(_mla_head_cols(mla_w_uk[0], MLA_HEADS, MLA_NOPE, 0))
    w_uvt = _bf(mla_w_uv[0].T)
    w_out0 = mix0_w_out[0]
    w_oa = _bf(w_out0[:MLA_HEADS * MLA_V])
    w_op = _bf(w_out0[MLA_HEADS * MLA_V:])

    m0 = mods_of(0)
    q, k, vt, p = _in0(x, ctx, m0, w_in0, mla_g_q[0][None], w_uq, mla_g_kv[0][None], w_uk, w_uvt,
                       _mla_tables())
    mla = dict(n_heads=MLA_HEADS_PER_STEP, shared_kv=False, dv=MLA_V)
    o = _attention(q, k, vt, context=False, **mla)
    o_ctx = _attention(q, k, vt, context=True, **mla)
    xa = _out0(x, ctx, m0, o, o_ctx, p, _bf(pool_w[0]), pool_scale[0][None], w_oa, w_op)
    xa = _ffn(xa, m0, _bf(ffn_w_up[0]), ffn_conv_w[0], ffn_conv_b[0][None], _bf(ffn_w_down[0]),
              g_final[None], final_norm=False)

    perm = _deinterleave(GQA_HEAD_DIM)
    n_qk = GQA_HEADS + GQA_KV_HEADS
    w_in = gqa_w_in[0]
    w_qk = w_in[:, :n_qk * LANES].reshape(D_MODEL, n_qk, LANES)[:, :, perm].reshape(D_MODEL, n_qk * LANES)
    w_vt = _bf(w_in[:, n_qk * LANES:].T)
    m1 = mods_of(1)
    q, k, vt = _in1(xa, m1, _bf(w_qk), w_vt, gqa_g_q[0][perm][None], gqa_g_k[0][perm][None], _gqa_tables())
    o = _attention(q, k, vt, n_heads=GQA_GROUP, shared_kv=True, dv=GQA_HEAD_DIM, context=False)
    xa = _out1(xa, m1, o, _bf(gqa_w_out[0]))
    return _ffn(xa, m1, _bf(ffn_w_up[1]), ffn_conv_w[1], ffn_conv_b[1][None], _bf(ffn_w_down[1]),
                g_final[None], final_norm=True)
```

```python
import functools

import jax
import jax.numpy as jnp
import numpy as np
from jax import lax
from jax.experimental import pallas as pl
from jax.experimental.pallas import tpu as pltpu

D_MODEL = 1024
SEQ = 8192
GRID_W = 64
CTX_LEN = 256
T_ALL = SEQ + CTX_LEN
ROPE_THETA = 10000.0
EPS = 1e-6
N_MOD = 6
MLA_HEADS = 8
MLA_NOPE = 64
MLA_ROPE = 32
MLA_V = 64
MLA_Q_RANK = 384
MLA_KV_RANK = 256
POOL_WINDOWS = (2, 4, 8, 16)
POOL_WIDTH = 512
GQA_HEADS = 8
GQA_KV_HEADS = 2
GQA_HEAD_DIM = 128
GQA_GROUP = GQA_HEADS // GQA_KV_HEADS
D_FF = 2816
CONV_W = 3

LANES = 128
SUBLANES = 8
ROW_TILE = 256
N_LAT_TILES = SEQ // ROW_TILE
N_TILES = T_ALL // ROW_TILE
HALO_BLOCKS_PER_TILE = ROW_TILE // SUBLANES
KV_CHUNK = 768
PAIRS_PER_TRIP = 2
TILES_PER_STEP = 4
FF_CHUNK = 256
PAIR = 2
PROJ_PAIR = 4
V7X_VMEM_BYTES = 64 * 1024 * 1024
VMEM_LIMIT = V7X_VMEM_BYTES * 7 // 8

MIX0_IN_PAD = MLA_Q_RANK + MLA_KV_RANK + LANES + POOL_WIDTH
HEADS_W = MLA_HEADS * LANES
LOG2_E = 1.4426950408889634
ONES_ROWS = 16
MLA_VROWS = MLA_V + ONES_ROWS
GQA_VROWS = GQA_HEAD_DIM + ONES_ROWS
MLA_HEADS_PER_STEP = 4


def _params(n_axes):
    return pltpu.CompilerParams(dimension_semantics=("arbitrary",) * n_axes, vmem_limit_bytes=VMEM_LIMIT)


def _rms(x):
    return x * lax.rsqrt(jnp.mean(x * x, axis=-1, keepdims=True) + EPS)


def _dot(a, b):
    return jnp.dot(a, b, preferred_element_type=jnp.float32)


def _dot_nt(a, b):
    return lax.dot_general(a, b, (((1,), (1,)), ((), ())), preferred_element_type=jnp.float32)


def _bf(x):
    return x.astype(jnp.bfloat16)


def _mod_kernel(c_ref, w_ref, b_ref, o_ref):
    s = jax.nn.silu(c_ref[...])
    o_ref[0] = jnp.dot(s, w_ref[0], preferred_element_type=jnp.float32,
                       precision=lax.Precision.HIGHEST) + b_ref[0]


def _modulation(cc, w_mod, b_mod):
    depth = w_mod.shape[0]
    return pl.pallas_call(
        _mod_kernel,
        out_shape=jax.ShapeDtypeStruct((depth, SUBLANES, N_MOD * D_MODEL), jnp.float32),
        grid=(depth, N_MOD),
        in_specs=[
            pl.BlockSpec((SUBLANES, D_MODEL), lambda i, j: (0, 0)),
            pl.BlockSpec((1, D_MODEL, D_MODEL), lambda i, j: (i, 0, j)),
            pl.BlockSpec((1, 1, D_MODEL), lambda i, j: (i, 0, j)),
        ],
        out_specs=pl.BlockSpec((1, SUBLANES, D_MODEL), lambda i, j: (i, 0, j)),
        compiler_params=_params(2),
        name="modulation",
    )(cc, w_mod, b_mod.reshape(depth, 1, N_MOD * D_MODEL))


def _row_spec(width, pair=PAIR):
    return pl.BlockSpec((pair, ROW_TILE, width), lambda b, t: (b, t, 0))


def _mod_spec(pair=PAIR):
    return pl.BlockSpec((pair, 1, N_MOD, D_MODEL), lambda b, t: (b, t // N_LAT_TILES, 0, 0))


def _const_spec(shape):
    nd = len(shape)
    return pl.BlockSpec(shape, lambda b, t: (0,) * nd, pipeline_mode=pl.Buffered(1))


def _table_spec():
    return pl.BlockSpec((ROW_TILE, LANES), lambda b, t: (t, 0))


def _prev_halo_spec(width, pair=PAIR):
    return pl.BlockSpec((pair, SUBLANES, width),
                        lambda b, t: (b, jnp.maximum(t * HALO_BLOCKS_PER_TILE - 1, 0), 0))


def _next_halo_spec(width, total_rows=T_ALL, pair=PAIR):
    last = total_rows // SUBLANES - 1
    return pl.BlockSpec((pair, SUBLANES, width),
                        lambda b, t: (b, jnp.minimum((t + 1) * HALO_BLOCKS_PER_TILE, last), 0))


def _halo_valid(t):
    prev_ok = jnp.logical_and(t != 0, t != N_LAT_TILES)
    next_ok = jnp.logical_and(t != N_LAT_TILES - 1, t != N_LAT_TILES)
    return prev_ok, next_ok


def _with_halo(prev_ref, main_ref, next_ref, s):
    return jnp.concatenate([prev_ref[s], main_ref[s], next_ref[s]], axis=0)


def _halo_row_mask(prev_ok, next_ok):
    r = lax.broadcasted_iota(jnp.int32, (ROW_TILE + 2 * SUBLANES, 1), 0)
    return jnp.logical_and(jnp.logical_or(r >= SUBLANES, prev_ok),
                           jnp.logical_or(r < ROW_TILE + SUBLANES, next_ok))


def _shift_rows(x, k):
    n = x.shape[0]
    shifted = x if k == 0 else pltpu.roll(x, k % n, axis=0)
    return shifted[SUBLANES:SUBLANES + ROW_TILE]


def _rope_mla(x, c, s_up, s_dn):
    return x * c + pltpu.roll(x, MLA_ROPE // 2, axis=1) * s_up + pltpu.roll(x, LANES - MLA_ROPE // 2, axis=1) * s_dn


def _two_stream_specs(width=D_MODEL, pair=PAIR):
    return [pl.BlockSpec((pair, ROW_TILE, width), lambda b, t: (b, jnp.minimum(t, N_LAT_TILES - 1), 0)),
            pl.BlockSpec((pair, CTX_LEN, width), lambda b, t: (b, 0, 0))]


def _two_stream_rows(x_ref, ctx_ref, s):
    return jnp.where(pl.program_id(1) == N_LAT_TILES, ctx_ref[s], x_ref[s])


def _in0_kernel(x_ref, ctx_ref, mod_ref, w_in_ref, gq_ref, w_uq_ref, gkv_ref, w_uk_ref, w_uvt_ref,
                c_ref, su_ref, sd_ref, q_ref, k_ref, v_ref, p_ref):
    i_kv = MLA_Q_RANK
    i_kr = i_kv + MLA_KV_RANK
    i_p = i_kr + LANES
    c, su, sd = c_ref[...], su_ref[...], sd_ref[...]
    scale = (MLA_NOPE + MLA_ROPE) ** -0.5 * LOG2_E
    pair = range(x_ref.shape[0])

    def modulated(s):
        m = mod_ref[s, 0]
        return _bf(_rms(_two_stream_rows(x_ref, ctx_ref, s)) * (1.0 + m[1:2]) + m[0:1])

    ones = jnp.ones((ONES_ROWS, ROW_TILE), jnp.bfloat16)
    for s in pair:
        a = _dot(modulated(s), w_in_ref[...])
        q = _dot(_bf(_rms(a[:, :i_kv]) * gq_ref[...]), w_uq_ref[...])
        ckv = _bf(_rms(a[:, i_kv:i_kr]) * gkv_ref[...])
        kn = _dot(ckv, w_uk_ref[...])
        vt = _dot_nt(w_uvt_ref[...], ckv)
        kr = _rope_mla(a[:, i_kr:i_p], c, su, sd)
        for hd in range(MLA_HEADS):
            sl = slice(hd * LANES, (hd + 1) * LANES)
            q_ref[s, :, sl] = _bf(_rope_mla(q[:, sl], c, su, sd) * scale)
            k_ref[s, :, sl] = _bf(kn[:, sl] + kr)
            v_ref[s, hd * MLA_VROWS:hd * MLA_VROWS + MLA_V, :] = _bf(vt[hd * MLA_V:(hd + 1) * MLA_V])
            v_ref[s, hd * MLA_VROWS + MLA_V:(hd + 1) * MLA_VROWS, :] = ones
        p_ref[s] = a[:, i_p:]


def _vt_spec(rows, pair=PAIR):
    return pl.BlockSpec((pair, rows, ROW_TILE), lambda b, t: (b, 0, t))


def _in0(x, ctx, mods, w_in, g_q, w_uq, g_kv, w_uk, w_uvt, tabs):
    B = x.shape[0]
    assert w_in.shape == (D_MODEL, MIX0_IN_PAD)
    outs = (
        jax.ShapeDtypeStruct((B, T_ALL, HEADS_W), jnp.bfloat16),
        jax.ShapeDtypeStruct((B, T_ALL, HEADS_W), jnp.bfloat16),
        jax.ShapeDtypeStruct((B, MLA_HEADS * MLA_VROWS, T_ALL), jnp.bfloat16),
        jax.ShapeDtypeStruct((B, T_ALL, POOL_WIDTH), jnp.float32),
    )
    return pl.pallas_call(
        _in0_kernel,
        out_shape=outs,
        grid=(B // PROJ_PAIR, N_TILES),
        in_specs=_two_stream_specs(pair=PROJ_PAIR) + [
            _mod_spec(PROJ_PAIR),
            _const_spec(w_in.shape), _const_spec(g_q.shape), _const_spec(w_uq.shape),
            _const_spec(g_kv.shape), _const_spec(w_uk.shape), _const_spec(w_uvt.shape),
            _table_spec(), _table_spec(), _table_spec(),
        ],
        out_specs=(_row_spec(HEADS_W, PROJ_PAIR), _row_spec(HEADS_W, PROJ_PAIR),
                   _vt_spec(MLA_HEADS * MLA_VROWS, PROJ_PAIR), _row_spec(POOL_WIDTH, PROJ_PAIR)),
        compiler_params=_params(2),
        name="mla_pool_in_proj",
    )(x, ctx, mods, w_in, g_q, w_uq, g_kv, w_uk, w_uvt, *tabs)


def _in1_kernel(x_ref, mod_ref, w_in_ref, w_vt_ref, gq_ref, gk_ref, c_ref, s_ref, q_ref, k_ref, v_ref):
    c, sn = c_ref[...], s_ref[...]
    scale = GQA_HEAD_DIM ** -0.5 * LOG2_E
    pair = range(x_ref.shape[0])

    def modulated(s):
        m = mod_ref[s, 0]
        return _bf(_rms(x_ref[s]) * (1.0 + m[1:2]) + m[0:1])

    ones = jnp.ones((ONES_ROWS, ROW_TILE), jnp.bfloat16)

    for s in pair:
        h = modulated(s)
        a = _dot(h, w_in_ref[...])
        vt = _dot_nt(w_vt_ref[...], h)

        def head(j, gain, a=a):
            xn = _rms(a[:, j * LANES:(j + 1) * LANES]) * gain
            return xn * c + pltpu.roll(xn, LANES // 2, axis=1) * sn

        for j in range(GQA_HEADS):
            q_ref[s, :, j * LANES:(j + 1) * LANES] = _bf(head(j, gq_ref[...]) * scale)
        for j in range(GQA_KV_HEADS):
            k_ref[s, :, j * LANES:(j + 1) * LANES] = _bf(head(GQA_HEADS + j, gk_ref[...]))
            v_ref[s, j * GQA_VROWS:j * GQA_VROWS + GQA_HEAD_DIM, :] = _bf(vt[j * GQA_HEAD_DIM:(j + 1) * GQA_HEAD_DIM])
            v_ref[s, j * GQA_VROWS + GQA_HEAD_DIM:(j + 1) * GQA_VROWS, :] = ones


def _in1(xa, mods, w_in, w_vt, g_q, g_k, tabs):
    B = xa.shape[0]
    kv_w = GQA_KV_HEADS * LANES
    outs = (
        jax.ShapeDtypeStruct((B, T_ALL, HEADS_W), jnp.bfloat16),
        jax.ShapeDtypeStruct((B, T_ALL, kv_w), jnp.bfloat16),
        jax.ShapeDtypeStruct((B, GQA_KV_HEADS * GQA_VROWS, T_ALL), jnp.bfloat16),
    )
    return pl.pallas_call(
        _in1_kernel,
        out_shape=outs,
        grid=(B // PAIR, N_TILES),
        in_specs=[
            _row_spec(D_MODEL), _mod_spec(), _const_spec(w_in.shape), _const_spec(w_vt.shape),
            _const_spec(g_q.shape), _const_spec(g_k.shape), _table_spec(), _table_spec(),
        ],
        out_specs=(_row_spec(HEADS_W), _row_spec(kv_w), _vt_spec(GQA_KV_HEADS * GQA_VROWS)),
        compiler_params=_params(2),
        name="gqa_in_proj",
    )(xa, mods, w_in, w_vt, g_q, g_k, *tabs)


def _score(q, k):
    return _dot_nt(k, q)


def _col_max(s):
    return jnp.max(s, axis=0, keepdims=True)


def _accumulate(s, cm, vt, state):
    m, acc = state
    m_new = jnp.maximum(m, cm)
    p = _bf(jnp.exp2(s - m_new))
    return m_new, jnp.exp2(m - m_new) * acc + _dot(vt, p)


def _attn_views(k_ref, vt_ref, *, shared_kv, vrows):
    def k_of(hd, rows):
        j = 0 if shared_kv else hd
        return k_ref[0, rows, j * LANES:(j + 1) * LANES]

    def vt_of(hd, rows):
        j = 0 if shared_kv else hd
        return vt_ref[0, j * vrows:(j + 1) * vrows, rows]

    return k_of, vt_of


def _attn_init(n_heads, vrows, tq):
    return tuple((jnp.full((1, tq), -jnp.inf, jnp.float32), jnp.zeros((vrows, tq), jnp.float32))
                 for _ in range(n_heads))


def _attn_output(states, dv):
    ot = jnp.concatenate([acc[:dv] / acc[dv:dv + 1] for _, acc in states], axis=0)
    return _bf(ot.T)


def _attn_kernel(q_ref, k_ref, vt_ref, o_ref, *s_refs, n_heads, shared_kv, dv):
    vrows = dv + ONES_ROWS
    heads = range(n_heads)
    n_chunks = T_ALL // KV_CHUNK
    assert n_chunks % 2 == 1 and (n_chunks - 3) % (2 * PAIRS_PER_TRIP) == 0
    k_of, vt_of = _attn_views(k_ref, vt_ref, shared_kv=shared_kv, vrows=vrows)

    def chunk(c):
        return pl.ds(pl.multiple_of(c * KV_CHUNK, KV_CHUNK), KV_CHUNK)

    def qs_of(tile):
        return [q_ref[0, tile * ROW_TILE:(tile + 1) * ROW_TILE, hd * LANES:(hd + 1) * LANES] for hd in heads]

    def park(qs, c, hd, buf):
        s = _score(qs[hd], k_of(hd, chunk(c)))
        buf[hd][...] = s
        return _col_max(s)

    def consume(c, hd, buf, cm, state):
        return _accumulate(buf[hd][...], cm, vt_of(hd, chunk(c)), state)

    def half_step(qs, c_next, c, cur, nxt, cms, states):
        new_cms, new_states = [], []
        for hd in heads:
            new_cms.append(park(qs, c_next, hd, nxt))
            new_states.append(consume(c, hd, cur, cms[hd], states[hd]))
        return tuple(new_cms), tuple(new_states)

    even, odd = s_refs[:n_heads], s_refs[n_heads:]
    cms = tuple(park(qs_of(0), 0, hd, even) for hd in heads)
    for tile in range(TILES_PER_STEP):
        qs = qs_of(tile)

        def pair(c, carry, qs=qs, even=even, odd=odd):
            carry = half_step(qs, c + 1, c, odd, even, *carry)
            return half_step(qs, c + 2, c + 1, even, odd, *carry)

        def body(i, carry, pair=pair):
            for k in range(PAIRS_PER_TRIP):
                carry = pair(1 + 2 * (PAIRS_PER_TRIP * i + k), carry)
            return carry

        carry = half_step(qs, 1, 0, even, odd, cms, _attn_init(n_heads, vrows, ROW_TILE))
        carry = lax.fori_loop(0, (n_chunks - 3) // (2 * PAIRS_PER_TRIP), body, carry)
        cms, states = half_step(qs, n_chunks - 1, n_chunks - 2, odd, even, *carry)
        if tile + 1 < TILES_PER_STEP:
            cms, states = half_step(qs_of(tile + 1), 0, n_chunks - 1, even, odd, cms, states)
            even, odd = odd, even
        else:
            states = [consume(n_chunks - 1, hd, even, cms[hd], states[hd]) for hd in heads]
        o_ref[0, tile * ROW_TILE:(tile + 1) * ROW_TILE, :] = _attn_output(states, dv)


def _ctx_attn_kernel(q_ref, k_ref, vt_ref, o_ref, *, n_heads, shared_kv, dv):
    vrows = dv + ONES_ROWS
    k_of, vt_of = _attn_views(k_ref, vt_ref, shared_kv=shared_kv, vrows=vrows)
    rows = slice(None)
    states = []
    for hd, state in enumerate(_attn_init(n_heads, vrows, CTX_LEN)):
        s = _score(q_ref[0, :, hd * LANES:(hd + 1) * LANES], k_of(hd, rows))
        states.append(_accumulate(s, _col_max(s), vt_of(hd, rows), state))
    o_ref[0] = _attn_output(states, dv)


def _attention(q, k, vt, *, n_heads, shared_kv, dv, context):
    B = q.shape[0]
    groups = q.shape[2] // (n_heads * LANES)
    kv_heads = 1 if shared_kv else n_heads
    vrows = dv + ONES_ROWS
    params = dict(n_heads=n_heads, shared_kv=shared_kv, dv=dv)
    if context:
        last = T_ALL // CTX_LEN - 1
        return pl.pallas_call(
            functools.partial(_ctx_attn_kernel, **params),
            out_shape=jax.ShapeDtypeStruct((B, CTX_LEN, groups * n_heads * dv), jnp.bfloat16),
            grid=(B, groups),
            in_specs=[pl.BlockSpec((1, CTX_LEN, n_heads * LANES), lambda b, g: (b, last, g)),
                      pl.BlockSpec((1, CTX_LEN, kv_heads * LANES), lambda b, g: (b, last, g)),
                      pl.BlockSpec((1, kv_heads * vrows, CTX_LEN), lambda b, g: (b, g, last))],
            out_specs=pl.BlockSpec((1, CTX_LEN, n_heads * dv), lambda b, g: (b, 0, g)),
            compiler_params=_params(2),
            name="context_attention",
        )(q, k, vt)
    rows = TILES_PER_STEP * ROW_TILE
    return pl.pallas_call(
        functools.partial(_attn_kernel, **params),
        out_shape=jax.ShapeDtypeStruct((B, SEQ, groups * n_heads * dv), jnp.bfloat16),
        grid=(B, groups, SEQ // rows),
        in_specs=[pl.BlockSpec((1, rows, n_heads * LANES), lambda b, g, i: (b, i, g)),
                  pl.BlockSpec((1, T_ALL, kv_heads * LANES), lambda b, g, i: (b, 0, g)),
                  pl.BlockSpec((1, kv_heads * vrows, T_ALL), lambda b, g, i: (b, g, 0))],
        out_specs=pl.BlockSpec((1, rows, n_heads * dv), lambda b, g, i: (b, i, g)),
        scratch_shapes=[pltpu.VMEM((KV_CHUNK, ROW_TILE), jnp.float32)] * (2 * n_heads),
        compiler_params=_params(3),
        name="attention",
    )(q, k, vt)


def _out0_kernel(x_ref, ctx_ref, mod_ref, o_ref, oc_ref, pp_ref, p_ref, pn_ref, w_pool_ref, s_pool_ref,
                 w_oa_ref, w_op_ref, y_ref):
    t = pl.program_id(1)
    row_ok = _halo_row_mask(*_halo_valid(t))
    t0 = jnp.where(t == N_LAT_TILES, 0, t * ROW_TILE)
    t_len = jnp.where(t == N_LAT_TILES, CTX_LEN, SEQ)
    pos = t0 + lax.broadcasted_iota(jnp.int32, (ROW_TILE, 1), 0)
    pair = range(x_ref.shape[0])
    groups = range(len(POOL_WINDOWS))

    def pool_inputs(s):
        pe = jnp.where(row_ok, _with_halo(pp_ref, p_ref, pn_ref, s), 0.0)
        out = []
        for g, w in enumerate(POOL_WINDOWS):
            run = pe[:, g * LANES:(g + 1) * LANES]
            span = 1
            while span < w:
                run = run + pltpu.roll(run, run.shape[0] - span, axis=0)
                span *= 2
            win = _shift_rows(run, w // 2)
            lo = jnp.clip(pos - w // 2, 0, t_len)
            hi = jnp.clip(pos - w // 2 + w, 0, t_len)
            out.append(_bf(win / (hi - lo).astype(jnp.float32) - p_ref[s, :, g * LANES:(g + 1) * LANES]))
        return out

    d = [pool_inputs(s) for s in pair]
    pooled = [[_dot(d[s][g], w_pool_ref[g]) for s in pair] for g in groups]
    pooled = [_bf(jnp.concatenate([pooled[g][s] for g in groups], axis=-1) * s_pool_ref[...]) for s in pair]
    y = [_dot(_two_stream_rows(o_ref, oc_ref, s), w_oa_ref[...]) + _dot(pooled[s], w_op_ref[...]) for s in pair]
    for s in pair:
        y_ref[s] = _two_stream_rows(x_ref, ctx_ref, s) + mod_ref[s, 0][2:3] * y[s]


def _out0(x, ctx, mods, o, o_ctx, p, w_pool, s_pool, w_oa, w_op):
    B = x.shape[0]
    return pl.pallas_call(
        _out0_kernel,
        out_shape=jax.ShapeDtypeStruct((B, T_ALL, D_MODEL), jnp.float32),
        grid=(B // PROJ_PAIR, N_TILES),
        in_specs=_two_stream_specs(pair=PROJ_PAIR) + [_mod_spec(PROJ_PAIR)]
        + _two_stream_specs(o.shape[2], PROJ_PAIR) + [
            _prev_halo_spec(POOL_WIDTH, PROJ_PAIR), _row_spec(POOL_WIDTH, PROJ_PAIR),
            _next_halo_spec(POOL_WIDTH, pair=PROJ_PAIR),
            _const_spec(w_pool.shape), _const_spec(s_pool.shape),
            _const_spec(w_oa.shape), _const_spec(w_op.shape),
        ],
        out_specs=_row_spec(D_MODEL, PROJ_PAIR),
        compiler_params=_params(2),
        name="mla_pool_out_proj",
    )(x, ctx, mods, o, o_ctx, p, p, p, w_pool, s_pool, w_oa, w_op)


def _out1_kernel(x_ref, mod_ref, o_ref, w_ref, y_ref):
    pair = range(x_ref.shape[0])
    y = [_dot(o_ref[s], w_ref[...]) for s in pair]
    for s in pair:
        y_ref[s] = x_ref[s] + mod_ref[s, 0][2:3] * y[s]


def _out1(xa, mods, o, w_out):
    B = xa.shape[0]
    return pl.pallas_call(
        _out1_kernel,
        out_shape=jax.ShapeDtypeStruct((B, SEQ, D_MODEL), jnp.float32),
        grid=(B // PROJ_PAIR, N_LAT_TILES),
        in_specs=[_row_spec(D_MODEL, PROJ_PAIR), _mod_spec(PROJ_PAIR), _row_spec(HEADS_W, PROJ_PAIR),
                  _const_spec(w_out.shape)],
        out_specs=_row_spec(D_MODEL, PROJ_PAIR),
        compiler_params=_params(2),
        name="gqa_out_proj",
    )(xa, mods, o, w_out)


def _ffn_kernel(xp_ref, x_ref, xn_ref, mod_ref, w_up_ref, cw_ref, cb_ref, w_dn_ref, gf_ref, y_ref,
                *, final_norm):
    row_ok = _halo_row_mask(*_halo_valid(pl.program_id(1)))
    pair = range(x_ref.shape[0])

    def modulated(s):
        m = mod_ref[s, 0]
        he = _rms(_with_halo(xp_ref, x_ref, xn_ref, s)) * (1.0 + m[4:5]) + m[3:4]
        return _bf(jnp.where(row_ok, he, 0.0))

    he = [modulated(s) for s in pair]

    def up(s, cs):
        g = _dot(he[s], w_up_ref[:, cs])
        u = _dot(he[s][SUBLANES:SUBLANES + ROW_TILE], w_up_ref[:, D_FF + cs.start:D_FF + cs.stop])
        return g, u

    def down(cs, g, u):
        acc = cb_ref[:, cs]
        for j in range(CONV_W):
            acc = acc + _shift_rows(g, CONV_W // 2 - j) * cw_ref[j:j + 1, cs]
        return _dot(_bf(jax.nn.silu(acc) * u), w_dn_ref[cs, :])

    chunks = [slice(c0, c0 + FF_CHUNK) for c0 in range(0, D_FF, FF_CHUNK)]
    y = [jnp.zeros((ROW_TILE, D_MODEL), jnp.float32) for _ in pair]
    pending = [up(s, chunks[0]) for s in pair]
    for cs, cs_next in zip(chunks, chunks[1:] + [None]):
        ahead = [up(s, cs_next) for s in pair] if cs_next is not None else None
        y = [y[s] + down(cs, *pending[s]) for s in pair]
        pending = ahead
    for s in pair:
        out = x_ref[s] + mod_ref[s, 0][5:6] * y[s]
        if final_norm:
            out = _rms(out) * gf_ref[...]
        y_ref[s] = out


def _ffn(xa, mods, w_up, conv_w, conv_b, w_dn, g_final, *, final_norm):
    B = xa.shape[0]
    rows = xa.shape[1]
    n_t = rows // ROW_TILE
    assert final_norm == (rows == SEQ) and conv_w.shape == (CONV_W, D_FF)
    return pl.pallas_call(
        functools.partial(_ffn_kernel, final_norm=final_norm),
        out_shape=jax.ShapeDtypeStruct((B, rows, D_MODEL), jnp.float32),
        grid=(B // PAIR, n_t),
        in_specs=[
            _prev_halo_spec(D_MODEL), _row_spec(D_MODEL), _next_halo_spec(D_MODEL, rows), _mod_spec(),
            _const_spec(w_up.shape), _const_spec(conv_w.shape), _const_spec(conv_b.shape),
            _const_spec(w_dn.shape), _const_spec(g_final.shape),
        ],
        out_specs=_row_spec(D_MODEL),
        compiler_params=_params(2),
        name="conv_ffn",
    )(xa, xa, xa, mods, w_up, conv_w, conv_b, w_dn, g_final)


def _rope_angles(rope_dim):
    rows = SEQ // GRID_W
    row = np.repeat(np.arange(rows, dtype=np.float32), GRID_W)
    col = np.tile(np.arange(GRID_W, dtype=np.float32), rows)
    n_freq = rope_dim // 4
    freq = (np.float32(ROPE_THETA) ** (-np.arange(n_freq, dtype=np.float32) / n_freq)).astype(np.float32)
    ang = np.concatenate([row[:, None] * freq, col[:, None] * freq], axis=-1)
    return np.concatenate([ang, np.zeros((CTX_LEN, rope_dim // 2), np.float32)], axis=0).astype(np.float32)


def _mla_tables():
    ang = _rope_angles(MLA_ROPE)
    cos, sin = np.cos(ang), np.sin(ang)
    half = MLA_ROPE // 2
    ones = np.ones((T_ALL, MLA_NOPE), np.float32)
    zn = np.zeros((T_ALL, MLA_NOPE), np.float32)
    zh = np.zeros((T_ALL, half), np.float32)
    zt = np.zeros((T_ALL, LANES - MLA_NOPE - MLA_ROPE), np.float32)
    c = np.concatenate([ones, cos, cos, zt], axis=-1)
    s_up = np.concatenate([zn, zh, sin, zt], axis=-1)
    s_dn = np.concatenate([zn, -sin, zh, zt], axis=-1)
    return jnp.asarray(c), jnp.asarray(s_up), jnp.asarray(s_dn)


def _gqa_tables():
    ang = _rope_angles(GQA_HEAD_DIM)
    cos, sin = np.cos(ang), np.sin(ang)
    return jnp.asarray(np.concatenate([cos, cos], axis=-1)), jnp.asarray(np.concatenate([-sin, sin], axis=-1))


def _mla_head_cols(w, n_heads, nope, rope):
    K = w.shape[0]
    w = w.reshape(K, n_heads, nope + rope)
    parts = [w[..., :nope]]
    if rope:
        parts += [w[..., nope::2], w[..., nope + 1::2]]
    parts.append(jnp.zeros((K, n_heads, LANES - nope - rope), w.dtype))
    return jnp.concatenate(parts, axis=-1).reshape(K, n_heads * LANES)


def _deinterleave(n):
    return np.concatenate([np.arange(0, n, 2), np.arange(1, n, 2)])


def kernel(x, c, ctx, c_ctx, w_mod, b_mod, mix0_w_in, mla_g_q, mla_w_uq, mla_g_kv, mla_w_uk, mla_w_uv,
           pool_w, pool_scale, mix0_w_out, gqa_w_in, gqa_g_q, gqa_g_k, gqa_w_out,
           ffn_w_up, ffn_conv_w, ffn_conv_b, ffn_w_down, g_final):
    B = x.shape[0]
    assert x.shape == (B, SEQ, D_MODEL) and ctx.shape == (B, CTX_LEN, D_MODEL)
    assert B < SUBLANES and B % PAIR == 0 and B % PROJ_PAIR == 0
    assert w_mod.shape[0] == 2

    cc = jnp.zeros((SUBLANES, D_MODEL), jnp.float32).at[:B].set(c).at[B].set(c_ctx)
    mod_all = _modulation(cc, w_mod, b_mod).reshape(2, SUBLANES, N_MOD, D_MODEL)

    def mods_of(i):
        lat = mod_all[i, :B]
        con = jnp.broadcast_to(mod_all[i, B], lat.shape)
        return jnp.stack([lat, con], axis=1)

    w_in = mix0_w_in[0]
    i_kv = MLA_Q_RANK
    i_kr = i_kv + MLA_KV_RANK
    i_p = i_kr + MLA_ROPE
    w_kr = _mla_head_cols(w_in[:, i_kr:i_p], 1, 0, MLA_ROPE)
    w_kr = jnp.roll(w_kr, MLA_NOPE, axis=1)
    w_in0 = _bf(jnp.concatenate([w_in[:, :i_kr], w_kr, w_in[:, i_p:]], axis=1))
    w_uq = _bf(_mla_head_cols(mla_w_uq[0], MLA_HEADS, MLA_NOPE, MLA_ROPE))
    w_uk = _bf(_mla_head_cols(mla_w_uk[0], MLA_HEADS, MLA_NOPE, 0))
    w_uvt = _bf(mla_w_uv[0].T)
    w_out0 = mix0_w_out[0]
    w_oa = _bf(w_out0[:MLA_HEADS * MLA_V])
    w_op = _bf(w_out0[MLA_HEADS * MLA_V:])

    m0 = mods_of(0)
    q, k, vt, p = _in0(x, ctx, m0, w_in0, mla_g_q[0][None], w_uq, mla_g_kv[0][None], w_uk, w_uvt,
                       _mla_tables())
    mla = dict(n_heads=MLA_HEADS_PER_STEP, shared_kv=False, dv=MLA_V)
    o = _attention(q, k, vt, context=False, **mla)
    o_ctx = _attention(q, k, vt, context=True, **mla)
    xa = _out0(x, ctx, m0, o, o_ctx, p, _bf(pool_w[0]), pool_scale[0][None], w_oa, w_op)
    xa = _ffn(xa, m0, _bf(ffn_w_up[0]), ffn_conv_w[0], ffn_conv_b[0][None], _bf(ffn_w_down[0]),
              g_final[None], final_norm=False)

    perm = _deinterleave(GQA_HEAD_DIM)
    n_qk = GQA_HEADS + GQA_KV_HEADS
    w_in = gqa_w_in[0]
    w_qk = w_in[:, :n_qk * LANES].reshape(D_MODEL, n_qk, LANES)[:, :, perm].reshape(D_MODEL, n_qk * LANES)
    w_vt = _bf(w_in[:, n_qk * LANES:].T)
    m1 = mods_of(1)
    q, k, vt = _in1(xa, m1, _bf(w_qk), w_vt, gqa_g_q[0][perm][None], gqa_g_k[0][perm][None], _gqa_tables())
    o = _attention(q, k, vt, n_heads=GQA_GROUP, shared_kv=True, dv=GQA_HEAD_DIM, context=False)
    xa = _out1(xa, m1, o, _bf(gqa_w_out[0]))
    return _ffn(xa, m1, _bf(ffn_w_up[1]), ffn_conv_w[1], ffn_conv_b[1][None], _bf(ffn_w_down[1]),
                g_final[None], final_norm=True)
```
